```python
import jax
import jax.numpy as jnp
from jax import lax
import numpy as np

D_MODEL = 1024
BATCH = 4
SEQ = 4096
DEPTH = 4
DEC_BATCH = 128
DEC_SEQ = 8
PAST_LEN = 2048
PAGE_SIZE = 128

HEAD_DIM = 64
MIX_WIDTH = D_MODEL
RWKV_WIDTH = MIX_WIDTH // 2
NSA_WIDTH = MIX_WIDTH - RWKV_WIDTH
RWKV_HEADS = RWKV_WIDTH // HEAD_DIM
NSA_HEADS = NSA_WIDTH // HEAD_DIM
NSA_KV_HEADS = 2
NSA_GROUP = NSA_HEADS // NSA_KV_HEADS
D_DECAY_LORA = 64
D_AAA_LORA = 64
D_GATE_LORA = 128
RW_PROJ = 3 * RWKV_WIDTH + D_DECAY_LORA + D_AAA_LORA + D_GATE_LORA
KV_BRANCH = 2 * NSA_KV_HEADS * HEAD_DIM
NSA_PROJ = NSA_WIDTH + 3 * KV_BRANCH + 3 * NSA_HEADS
IN_PROJ = RW_PROJ + NSA_PROJ
CMP_BLOCK = 32
CMP_STRIDE = 16
CMP_HIDDEN = 128
SEL_BLOCK = 64
N_SEL = 16
WINDOW = 512
Q_BLOCK = 128
ROPE_THETA = 10000.0
D_FF = 2816
N_EXPERTS = 8
TOP_K = 2
D_FF_EXPERT = 2816
N_DENSE = (DEPTH + 1) // 2
N_MOE = DEPTH // 2
RMS_EPS = 1e-6
GN_EPS = 64e-5
NEG = -1e30
FORCE_SCORE = 1e4

kernel_name = 'hymba_rwkv7_nsa_decoder_step'


def rmsnorm(x, g):
    xf = x.astype(jnp.float32)
    y = xf * lax.rsqrt(jnp.mean(xf * xf, axis=-1, keepdims=True) + RMS_EPS)
    return (y * g.astype(jnp.float32)).astype(x.dtype)


def rope(x, pos):
    half = HEAD_DIM // 2
    inv = ROPE_THETA ** (-jnp.arange(half, dtype=jnp.float32) / half)
    ang = pos.astype(jnp.float32)[:, None] * inv[None, :]
    cos = jnp.cos(ang)[None, :, None, :]
    sin = jnp.sin(ang)[None, :, None, :]
    xf = x.astype(jnp.float32)
    x1, x2 = xf[..., :half], xf[..., half:]
    return jnp.concatenate([x1 * cos - x2 * sin, x2 * cos + x1 * sin], axis=-1).astype(x.dtype)


def masked_softmax(s, mask):
    return jax.nn.softmax(jnp.where(mask, s, NEG), axis=-1) * mask


def rwkv7_mix(p, shift_prev, wkv0, lw):
    B, T, _ = p.shape
    H, N, C = RWKV_HEADS, HEAD_DIM, RWKV_WIDTH
    p_prev = jnp.concatenate([shift_prev[:, None, :].astype(p.dtype), p[:, :-1]], axis=1)
    xs = p + (p_prev - p) * lw['rw_mu']
    o = 3 * C
    r, k, v = xs[..., :C], xs[..., C:2 * C], xs[..., 2 * C:o]
    w_lr = xs[..., o:o + D_DECAY_LORA]
    a_lr = xs[..., o + D_DECAY_LORA:o + D_DECAY_LORA + D_AAA_LORA]
    g_lr = xs[..., o + D_DECAY_LORA + D_AAA_LORA:]
    w = -jax.nn.softplus(-(lw['rw_w0'] + jnp.tanh(w_lr) @ lw['rw_w2']).astype(jnp.float32)) - 0.5
    decay = jnp.exp(-jnp.exp(w))
    a = jax.nn.sigmoid((lw['rw_a0'] + a_lr @ lw['rw_a2']).astype(jnp.float32))
    g = jax.nn.sigmoid(g_lr) @ lw['rw_g2']
    heads = lambda t: t.astype(jnp.float32).reshape(B, T, H, N)
    kk = heads(k * lw['rw_k_k'])
    kk = kk / jnp.maximum(jnp.sqrt(jnp.sum(kk * kk, axis=-1, keepdims=True)), 1e-12)
    a = heads(a)
    k = heads(k) * (1.0 + (a - 1.0) * lw['rw_k_a'].astype(jnp.float32).reshape(H, N))
    r, v, decay = heads(r), heads(v), heads(decay)

    def step(S, inp):
        r_t, w_t, k_t, v_t, kk_t, a_t = inp
        sa = jnp.einsum('bhij,bhj->bhi', S, -kk_t)
        S = (S * w_t[:, :, None, :] + sa[..., :, None] * (kk_t * a_t)[:, :, None, :]
             + v_t[..., :, None] * k_t[:, :, None, :])
        return S, jnp.einsum('bhij,bhj->bhi', S, r_t)

    seq = tuple(jnp.moveaxis(t, 1, 0) for t in (r, decay, k, v, kk, a))
    S_T, y = lax.scan(step, wkv0.astype(jnp.float32), seq)
    y = jnp.moveaxis(y, 0, 1)
    mu = jnp.mean(y, axis=-1, keepdims=True)
    var = jnp.mean(jnp.square(y - mu), axis=-1, keepdims=True)
    y = ((y - mu) * lax.rsqrt(var + GN_EPS)).reshape(B, T, C)
    y = y * lw['rw_ln_w'].astype(jnp.float32) + lw['rw_ln_b'].astype(jnp.float32)
    bonus = jnp.sum(r * k * lw['rw_r_k'].astype(jnp.float32), axis=-1, keepdims=True) * v
    out = (y + bonus.reshape(B, T, C)) * g.astype(jnp.float32)
    return out.astype(p.dtype), S_T


def nsa_project(pn, pos, lw):
    B, T, _ = pn.shape
    q = pn[..., :NSA_WIDTH].reshape(B, T, NSA_HEADS, HEAD_DIM)
    q = rope(rmsnorm(q, lw['nsa_q_gain']), pos)
    kv = pn[..., NSA_WIDTH:NSA_WIDTH + 3 * KV_BRANCH].reshape(B, T, 3, 2, NSA_KV_HEADS, HEAD_DIM)
    k = rmsnorm(kv[:, :, :, 0], lw['nsa_k_gain'][:, None, :])
    k = rope(k.reshape(B, T, 3 * NSA_KV_HEADS, HEAD_DIM), pos).reshape(B, T, 3, NSA_KV_HEADS, HEAD_DIM)
    kv = jnp.stack([k, kv[:, :, :, 1]], axis=3)
    gate = jax.nn.sigmoid(pn[..., NSA_WIDTH + 3 * KV_BRANCH:]).reshape(B, T, NSA_HEADS, 3)
    return q, kv, gate


def compress(rows, pe, w1, w2):
    B, T, G, D = rows.shape
    n_cmp = (T - CMP_BLOCK) // CMP_STRIDE + 1
    idx = np.arange(n_cmp)[:, None] * CMP_STRIDE + np.arange(CMP_BLOCK)[None, :]
    blk = rows[:, idx] + pe[:, None, :]
    blk = jnp.transpose(blk, (0, 1, 3, 2, 4)).reshape(B, n_cmp, G, CMP_BLOCK * D)
    return jax.nn.gelu(blk @ w1) @ w2


def to_sel_blocks(rows):
    B, T, G, D = rows.shape
    n_slc = -(-T // SEL_BLOCK)
    rows = jnp.pad(rows, ((0, 0), (0, n_slc * SEL_BLOCK - T), (0, 0), (0, 0)))
    return jnp.transpose(rows.reshape(B, n_slc, SEL_BLOCK, G, D), (0, 3, 1, 2, 4))


def sel_overlap(n_cmp, n_slc):
    start = np.arange(n_cmp) * CMP_STRIDE
    js = np.arange(n_slc) * SEL_BLOCK
    m = (start[:, None] < js[None, :] + SEL_BLOCK) & (start[:, None] + CMP_BLOCK > js[None, :])
    return jnp.asarray(m, jnp.float32)


def nsa_attend(q, q_pos, kc, vc, ks_blk, vs_blk, kw, vw, kw_pos, gate):
    B, Q = q.shape[:2]
    scale = HEAD_DIM ** -0.5
    qg = q.astype(jnp.float32).reshape(B, Q, NSA_KV_HEADS, NSA_GROUP, HEAD_DIM)
    n_cmp = kc.shape[1]
    cmp_end = jnp.arange(n_cmp) * CMP_STRIDE + CMP_BLOCK - 1
    m_c = cmp_end[None, :] <= q_pos[:, None]
    s_c = jnp.einsum('bqgrd,bngd->bgrqn', qg, kc.astype(jnp.float32)) * scale
    p_c = masked_softmax(s_c, m_c)
    o_c = jnp.einsum('bgrqn,bngd->bqgrd', p_c, vc.astype(jnp.float32))
    n_slc = ks_blk.shape[2]
    imp = jnp.einsum('bgrqn,nj->bgqj', p_c, sel_overlap(n_cmp, n_slc))
    j = jnp.arange(n_slc)
    cur = q_pos // SEL_BLOCK
    valid = j[None, :] * SEL_BLOCK <= q_pos[:, None]
    forced = (j[None, :] == 0) | (j[None, :] == cur[:, None]) | (j[None, :] == cur[:, None] - 1)
    imp = jnp.where(valid, jnp.where(forced, FORCE_SCORE, imp), NEG)
    top_s, top_i = lax.top_k(imp, min(N_SEL, n_slc))
    n_top = top_i.shape[-1]
    b_ix = jnp.arange(B)[:, None, None, None]
    g_ix = jnp.arange(NSA_KV_HEADS)[None, :, None, None]
    ks = ks_blk[b_ix, g_ix, top_i].astype(jnp.float32)
    vs = vs_blk[b_ix, g_ix, top_i].astype(jnp.float32)
    tok_pos = top_i[..., None] * SEL_BLOCK + jnp.arange(SEL_BLOCK)
    m_s = (tok_pos <= q_pos[None, None, :, None, None]) & (top_s > -1.0)[..., None]
    m_s = m_s.reshape(B, NSA_KV_HEADS, 1, Q, n_top * SEL_BLOCK)
    s_s = jnp.einsum('bqgrd,bgqkld->bgrqkl', qg, ks).reshape(B, NSA_KV_HEADS, NSA_GROUP, Q, n_top * SEL_BLOCK) * scale
    p_s = masked_softmax(s_s, m_s)
    o_s = jnp.einsum('bgrqt,bgqtd->bqgrd', p_s, vs.reshape(B, NSA_KV_HEADS, Q, n_top * SEL_BLOCK, HEAD_DIM))
    dpos = q_pos[:, None] - kw_pos[None, :]
    m_w = (kw_pos[None, :] >= 0) & (dpos >= 0) & (dpos < WINDOW)
    s_w = jnp.einsum('bqgrd,bkgd->bgrqk', qg, kw.astype(jnp.float32)) * scale
    p_w = masked_softmax(s_w, m_w)
    o_w = jnp.einsum('bgrqk,bkgd->bqgrd', p_w, vw.astype(jnp.float32))
    gt = gate.astype(jnp.float32).reshape(B, Q, NSA_KV_HEADS, NSA_GROUP, 3)
    o = o_c * gt[..., 0:1] + o_s * gt[..., 1:2] + o_w * gt[..., 2:3]
    return o.reshape(B, Q, NSA_WIDTH).astype(q.dtype)


def token_mixers(h, pos, shift_prev, wkv0, past_c, past_s, win_past, lw):
    B, T, _ = h.shape
    proj = h @ lw['w_in']
    y_rw, wkv_new = rwkv7_mix(proj[..., :RW_PROJ], shift_prev, wkv0, lw)
    shift_new = proj[:, -1, :RW_PROJ]
    q, kv, gate = nsa_project(proj[..., RW_PROJ:], pos, lw)
    kv_c, kv_s, kv_w = kv[:, :, 0], kv[:, :, 1], kv[:, :, 2]
    full_c = kv_c if past_c is None else jnp.concatenate([past_c.astype(kv_c.dtype), kv_c], axis=1)
    full_s = kv_s if past_s is None else jnp.concatenate([past_s.astype(kv_s.dtype), kv_s], axis=1)
    kc = compress(full_c[:, :, 0], lw['cmp_pe_k'], lw['cmp_w1_k'], lw['cmp_w2_k'])
    vc = compress(full_c[:, :, 1], lw['cmp_pe_v'], lw['cmp_w1_v'], lw['cmp_w2_v'])
    ks_blk = to_sel_blocks(full_s[:, :, 0])
    vs_blk = to_sel_blocks(full_s[:, :, 1])
    if win_past is None:
        kw_pad = jnp.pad(kv_w, ((0, 0), (WINDOW, 0), (0, 0), (0, 0), (0, 0)))
        n_qb = T // Q_BLOCK

        def q_block(args):
            qb, gb, i = args
            start = i * Q_BLOCK
            kw = lax.dynamic_slice_in_dim(kw_pad, start, WINDOW + Q_BLOCK, axis=1)
            return nsa_attend(qb, start + jnp.arange(Q_BLOCK), kc, vc, ks_blk, vs_blk,
                              kw[:, :, 0], kw[:, :, 1], start - WINDOW + jnp.arange(WINDOW + Q_BLOCK), gb)

        qs = q.reshape(B, n_qb, Q_BLOCK, NSA_HEADS, HEAD_DIM).swapaxes(0, 1)
        gs = gate.reshape(B, n_qb, Q_BLOCK, NSA_HEADS, 3).swapaxes(0, 1)
        o = lax.map(q_block, (qs, gs, jnp.arange(n_qb)))
        o = o.swapaxes(0, 1).reshape(B, T, NSA_WIDTH)
        win_new = kv_w[:, -min(WINDOW, T):]
    else:
        win_buf = win_past.shape[1]
        win_full = jnp.concatenate([win_past.astype(kv_w.dtype), kv_w], axis=1)
        kw_pos = pos[0] - win_buf + jnp.arange(win_buf + T)
        o = nsa_attend(q, pos, kc, vc, ks_blk, vs_blk, win_full[:, :, 0], win_full[:, :, 1], kw_pos, gate)
        win_new = win_full[:, -win_buf:]
    o = rmsnorm(o, lw['nsa_out_gain'])
    mix = jnp.concatenate([y_rw, o], axis=-1) @ lw['w_out']
    return mix, kv_c, kv_s, win_new, wkv_new.astype(h.dtype), shift_new


def swiglu(h, wg, wu, wd):
    return (jax.nn.silu(h @ wg) * (h @ wu)) @ wd


def moe_ffn(h, router, b_router, wg, wu, wd):
    logits = (h @ router).astype(jnp.float32) + b_router.astype(jnp.float32)
    top_v, top_i = lax.top_k(logits, TOP_K)
    gates = jax.nn.softmax(top_v, axis=-1)
    dense_gate = jnp.einsum('btk,btke->bte', gates, jax.nn.one_hot(top_i, N_EXPERTS, dtype=jnp.float32))
    out = jnp.zeros_like(h)
    for e in range(N_EXPERTS):
        out = out + swiglu(h, wg[e], wu[e], wd[e]) * dense_gate[..., e:e + 1].astype(h.dtype)
    return out


def setup_inputs(seed: int = 0) -> dict:
    key = jax.random.key(seed)
    ks = iter(jax.random.split(key, 64))
    nrm = lambda shape, scale: scale * jax.random.normal(next(ks), shape, jnp.float32)
    unif = lambda shape, lo, hi: jax.random.uniform(next(ks), shape, jnp.float32, lo, hi)
    n_pages = PAST_LEN // PAGE_SIZE
    n_used = DEC_BATCH * n_pages
    n_phys = n_used + (n_used + 3) // 4
    win_buf = min(WINDOW, PAST_LEN)
    page_table = jax.random.permutation(next(ks), n_phys)[:n_used].reshape(DEC_BATCH, n_pages).astype(jnp.int32)
    kv_pool = (DEPTH, n_phys, PAGE_SIZE, 2, NSA_KV_HEADS, HEAD_DIM)
    return {
        'x_prompt': nrm((BATCH, SEQ, D_MODEL), 1.0),
        'x_sample': nrm((DEC_BATCH, DEC_SEQ, D_MODEL), 1.0),
        'cache_cmp_kv': nrm(kv_pool, 1.0),
        'cache_sel_kv': nrm(kv_pool, 1.0),
        'state_win_kv': nrm((DEPTH, DEC_BATCH, win_buf, 2, NSA_KV_HEADS, HEAD_DIM), 1.0),
        'state_wkv': nrm((DEPTH, DEC_BATCH, RWKV_HEADS, HEAD_DIM, HEAD_DIM), 0.3),
        'state_shift': nrm((DEPTH, DEC_BATCH, RW_PROJ), 1.0),
        'page_table': page_table,
        'g_attn': 1.0 + nrm((DEPTH, D_MODEL), 0.02),
        'w_in': nrm((DEPTH, D_MODEL, IN_PROJ), D_MODEL ** -0.5),
        'w_out': nrm((DEPTH, MIX_WIDTH, D_MODEL), 0.5 * MIX_WIDTH ** -0.5),
        'rw_mu': unif((DEPTH, RW_PROJ), 0.0, 1.0),
        'rw_w0': unif((DEPTH, RWKV_WIDTH), -4.0, 0.0),
        'rw_w2': nrm((DEPTH, D_DECAY_LORA, RWKV_WIDTH), 0.5 * D_DECAY_LORA ** -0.5),
        'rw_a0': nrm((DEPTH, RWKV_WIDTH), 0.1),
        'rw_a2': nrm((DEPTH, D_AAA_LORA, RWKV_WIDTH), 0.5 * D_AAA_LORA ** -0.5),
        'rw_g2': nrm((DEPTH, D_GATE_LORA, RWKV_WIDTH), D_GATE_LORA ** -0.5),
        'rw_k_k': 0.85 + nrm((DEPTH, RWKV_WIDTH), 0.02),
        'rw_k_a': 1.0 + nrm((DEPTH, RWKV_WIDTH), 0.02),
        'rw_r_k': nrm((DEPTH, RWKV_HEADS, HEAD_DIM), 0.1),
        'rw_ln_w': 1.0 + nrm((DEPTH, RWKV_WIDTH), 0.02),
        'rw_ln_b': nrm((DEPTH, RWKV_WIDTH), 0.02),
        'nsa_q_gain': 1.0 + nrm((DEPTH, HEAD_DIM), 0.02),
        'nsa_k_gain': 1.0 + nrm((DEPTH, 3, HEAD_DIM), 0.02),
        'cmp_pe_k': nrm((DEPTH, CMP_BLOCK, HEAD_DIM), 0.1),
        'cmp_w1_k': nrm((DEPTH, CMP_BLOCK * HEAD_DIM, CMP_HIDDEN), (CMP_BLOCK * HEAD_DIM) ** -0.5),
        'cmp_w2_k': nrm((DEPTH, CMP_HIDDEN, HEAD_DIM), CMP_HIDDEN ** -0.5),
        'cmp_pe_v': nrm((DEPTH, CMP_BLOCK, HEAD_DIM), 0.1),
        'cmp_w1_v': nrm((DEPTH, CMP_BLOCK * HEAD_DIM, CMP_HIDDEN), (CMP_BLOCK * HEAD_DIM) ** -0.5),
        'cmp_w2_v': nrm((DEPTH, CMP_HIDDEN, HEAD_DIM), CMP_HIDDEN ** -0.5),
        'nsa_out_gain': 1.0 + nrm((DEPTH, NSA_WIDTH), 0.02),
        'g_ffn': 1.0 + nrm((DEPTH, D_MODEL), 0.02),
        'ffn_w_gate': nrm((N_DENSE, D_MODEL, D_FF), D_MODEL ** -0.5),
        'ffn_w_up': nrm((N_DENSE, D_MODEL, D_FF), D_MODEL ** -0.5),
        'ffn_w_down': nrm((N_DENSE, D_FF, D_MODEL), 0.5 * D_FF ** -0.5),
        'moe_router': nrm((N_MOE, D_MODEL, N_EXPERTS), D_MODEL ** -0.5),
        'moe_b_router': nrm((N_MOE, N_EXPERTS), 0.01),
        'moe_w_gate': nrm((N_MOE, N_EXPERTS, D_MODEL, D_FF_EXPERT), D_MODEL ** -0.5),
        'moe_w_up': nrm((N_MOE, N_EXPERTS, D_MODEL, D_FF_EXPERT), D_MODEL ** -0.5),
        'moe_w_down': nrm((N_MOE, N_EXPERTS, D_FF_EXPERT, D_MODEL), 0.5 * D_FF_EXPERT ** -0.5),
    }


def reference(x_prompt, x_sample, cache_cmp_kv, cache_sel_kv, state_win_kv, state_wkv, state_shift, page_table,
              g_attn, w_in, w_out, rw_mu, rw_w0, rw_w2, rw_a0, rw_a2, rw_g2, rw_k_k, rw_k_a, rw_r_k, rw_ln_w, rw_ln_b,
              nsa_q_gain, nsa_k_gain, cmp_pe_k, cmp_w1_k, cmp_w2_k, cmp_pe_v, cmp_w1_v, cmp_w2_v, nsa_out_gain,
              g_ffn, ffn_w_gate, ffn_w_up, ffn_w_down, moe_router, moe_b_router, moe_w_gate, moe_w_up, moe_w_down):
    Bp, Tp, _ = x_prompt.shape
    Bs, Ts, _ = x_sample.shape
    past_len = page_table.shape[1] * cache_cmp_kv.shape[2]
    pos_p = jnp.arange(Tp, dtype=jnp.int32)
    pos_s = past_len + jnp.arange(Ts, dtype=jnp.int32)
    shift0 = jnp.zeros((Bp, RW_PROJ), x_prompt.dtype)
    wkv0 = jnp.zeros((Bp, RWKV_HEADS, HEAD_DIM, HEAD_DIM), jnp.float32)
    xp, xs = x_prompt, x_sample
    st_p = ([], [], [], [], [])
    st_s = ([], [], [], [], [])
    for l in range(DEPTH):
        lw = {'w_in': w_in[l], 'w_out': w_out[l], 'rw_mu': rw_mu[l], 'rw_w0': rw_w0[l], 'rw_w2': rw_w2[l],
              'rw_a0': rw_a0[l], 'rw_a2': rw_a2[l], 'rw_g2': rw_g2[l], 'rw_k_k': rw_k_k[l], 'rw_k_a': rw_k_a[l],
              'rw_r_k': rw_r_k[l], 'rw_ln_w': rw_ln_w[l], 'rw_ln_b': rw_ln_b[l], 'nsa_q_gain': nsa_q_gain[l],
              'nsa_k_gain': nsa_k_gain[l], 'cmp_pe_k': cmp_pe_k[l], 'cmp_w1_k': cmp_w1_k[l], 'cmp_w2_k': cmp_w2_k[l],
              'cmp_pe_v': cmp_pe_v[l], 'cmp_w1_v': cmp_w1_v[l], 'cmp_w2_v': cmp_w2_v[l], 'nsa_out_gain': nsa_out_gain[l]}
        mix_p, *new_p = token_mixers(rmsnorm(xp, g_attn[l]), pos_p, shift0, wkv0, None, None, None, lw)
        past_c = cache_cmp_kv[l][page_table].reshape(Bs, past_len, 2, NSA_KV_HEADS, HEAD_DIM)
        past_s = cache_sel_kv[l][page_table].reshape(Bs, past_len, 2, NSA_KV_HEADS, HEAD_DIM)
        mix_s, *new_s = token_mixers(rmsnorm(xs, g_attn[l]), pos_s, state_shift[l], state_wkv[l],
                                     past_c, past_s, state_win_kv[l], lw)
        xp = xp + mix_p
        xs = xs + mix_s
        hp = rmsnorm(xp, g_ffn[l])
        hs = rmsnorm(xs, g_ffn[l])
        i = l // 2
        if l % 2 == 0:
            xp = xp + swiglu(hp, ffn_w_gate[i], ffn_w_up[i], ffn_w_down[i])
            xs = xs + swiglu(hs, ffn_w_gate[i], ffn_w_up[i], ffn_w_down[i])
        else:
            xp = xp + moe_ffn(hp, moe_router[i], moe_b_router[i], moe_w_gate[i], moe_w_up[i], moe_w_down[i])
            xs = xs + moe_ffn(hs, moe_router[i], moe_b_router[i], moe_w_gate[i], moe_w_up[i], moe_w_down[i])
        for buf, val in zip(st_p, new_p):
            buf.append(val)
        for buf, val in zip(st_s, new_s):
            buf.append(val)
    p_cmp, p_sel, p_win, p_wkv, p_shift = [jnp.stack(b, axis=0) for b in st_p]
    s_cmp, s_sel, s_win, s_wkv, s_shift = [jnp.stack(b, axis=0) for b in st_s]
    return (xp, xs, p_cmp, p_sel, p_win, p_wkv, p_shift, s_cmp, s_sel, s_win, s_wkv, s_shift)
```

```python
import functools
import math

import jax
import jax.numpy as jnp
import numpy as np
from jax import lax
from jax.experimental import pallas as pl
from jax.experimental.pallas import tpu as pltpu

F32 = jnp.float32
BF16 = jnp.bfloat16

HEAD_DIM = 64
RWKV_WIDTH = 512
NSA_WIDTH = 512
RWKV_HEADS = RWKV_WIDTH // HEAD_DIM
NSA_HEADS = NSA_WIDTH // HEAD_DIM
NSA_KV_HEADS = 2
NSA_GROUP = NSA_HEADS // NSA_KV_HEADS
D_DECAY_LORA = 64
D_AAA_LORA = 64
D_GATE_LORA = 128
RW_PROJ = 3 * RWKV_WIDTH + D_DECAY_LORA + D_AAA_LORA + D_GATE_LORA
KV_BRANCH = 2 * NSA_KV_HEADS * HEAD_DIM
N_GATE = 3 * NSA_HEADS
GATE_PAD = 128
PROJ_PAD = RW_PROJ + 3 * KV_BRANCH + NSA_WIDTH + GATE_PAD
CMP_BLOCK = 32
CMP_STRIDE = 16
CMP_HIDDEN = 128
SEL_BLOCK = 64
N_SEL = 16
WINDOW = 512
ROPE_THETA = 10000.0
N_EXPERTS = 8
RMS_EPS = 1e-6
GN_EPS = 64e-5
NEG = -1e30
FORCE_SCORE = 1e4
RWKV_CHUNK = 64
KEY_BLOCK = 128

VMEM_LIMIT_BYTES = 56 * 1024 * 1024

_NN = (((1,), (0,)), ((), ()))
_NT = (((1,), (1,)), ((), ()))
_TN = (((0,), (0,)), ((), ()))


def _dg(a, b, dn):
    return lax.dot_general(a, b, dn, preferred_element_type=F32)


def _mm(a, b, dn=_NN):
    return _dg(a.astype(BF16), b.astype(BF16), dn)


def _split(x):
    hi = x.astype(BF16)
    lo = (x - hi.astype(F32)).astype(BF16)
    return hi, lo


def _mm3(a, b, dn=_NN):
    ah, al = _split(a)
    bh, bl = _split(b)
    return _dg(ah, bh, dn) + (_dg(al, bh, dn) + _dg(ah, bl, dn))


def _mm2(a, b_exact, dn=_NN):
    ah, al = _split(a)
    bb = b_exact.astype(BF16)
    return _dg(ah, bb, dn) + _dg(al, bb, dn)


def _params(*sem):
    return pltpu.CompilerParams(dimension_semantics=sem, vmem_limit_bytes=VMEM_LIMIT_BYTES)


def _sigmoid(x):
    return 1.0 / (1.0 + jnp.exp(-x))


def _rms(x, g):
    return x * lax.rsqrt(jnp.mean(x * x, axis=-1, keepdims=True) + RMS_EPS) * g


def _norm_matmul_kernel(x_ref, g_ref, w_ref, o_ref):
    h = _rms(x_ref[...], g_ref[...])
    o_ref[...] = _dg(h.astype(BF16), w_ref[...], _NN)


def norm_matmul(x, g, w, tm):
    n, d = x.shape
    dout = w.shape[1]
    return pl.pallas_call(
        _norm_matmul_kernel,
        grid=(n // tm,),
        in_specs=[pl.BlockSpec((tm, d), lambda i: (i, 0)),
                  pl.BlockSpec((1, d), lambda i: (0, 0)),
                  pl.BlockSpec((d, dout), lambda i: (0, 0))],
        out_specs=pl.BlockSpec((tm, dout), lambda i: (i, 0)),
        out_shape=jax.ShapeDtypeStruct((n, dout), F32),
        compiler_params=_params("parallel"),
        name="norm_matmul",
    )(x, g, w)


def _rwkv_kernel(p_ref, shift_ref, wkv0_ref, mu_ref, w0_ref, w2_ref, a0_ref, a2_ref, g2_ref,
                 kk_ref, ka_ref, rk_ref, lnw_ref, lnb_ref,
                 y_ref, wkv_ref, shiftout_ref, s_scr, carry_scr, *, chunk, n_chunks):
    c = pl.program_id(1)
    C, N, H = chunk, HEAD_DIM, RWKV_HEADS

    @pl.when(c == 0)
    def _():
        s_scr[...] = wkv0_ref[...]
        carry_scr[...] = shift_ref[...]

    p = p_ref[...]
    row = lax.broadcasted_iota(jnp.int32, (C, 1), 0)
    p_prev = jnp.where(row == 0, carry_scr[...], pltpu.roll(p, 1, 0))
    carry_scr[...] = p[C - 1:C, :]
    xs = p + (p_prev - p) * mu_ref[...]
    o = 3 * RWKV_WIDTH
    r = xs[:, :RWKV_WIDTH]
    k = xs[:, RWKV_WIDTH:2 * RWKV_WIDTH]
    v = xs[:, 2 * RWKV_WIDTH:o]
    w_lr = xs[:, o:o + D_DECAY_LORA]
    a_lr = xs[:, o + D_DECAY_LORA:o + D_DECAY_LORA + D_AAA_LORA]
    g_lr = xs[:, o + D_DECAY_LORA + D_AAA_LORA:]
    z = -(w0_ref[...] + _mm3(jnp.tanh(w_lr), w2_ref[...]))
    softplus = jnp.maximum(z, 0.0) + jnp.log(1.0 + jnp.exp(-jnp.abs(z)))
    ld = -jnp.exp(-softplus - 0.5)
    a = _sigmoid(a0_ref[...] + _mm3(a_lr, a2_ref[...]))
    g = _mm3(_sigmoid(g_lr), g2_ref[...])
    kk_all = k * kk_ref[...]
    k_all = k * (1.0 + (a - 1.0) * ka_ref[...])

    ti = lax.broadcasted_iota(jnp.int32, (C, C), 0)
    si = lax.broadcasted_iota(jnp.int32, (C, C), 1)
    lower = (si <= ti)
    strict = (si < ti)
    tril = lower.astype(F32)
    eye = (si == ti).astype(F32)
    n_double = max(int(math.ceil(math.log2(C))) - 1, 0)

    for h in range(H):
        hs = slice(h * N, (h + 1) * N)
        rh, kh, vh, ah = r[:, hs], k_all[:, hs], v[:, hs], a[:, hs]
        kkh = kk_all[:, hs]
        kkh = kkh / jnp.maximum(jnp.sqrt(jnp.sum(kkh * kkh, axis=-1, keepdims=True)), 1e-12)
        bh = kkh * ah
        ldh = ld[:, hs]
        cum = _mm3(tril, ldh)
        w_in = jnp.exp(cum)
        w_ex = jnp.exp(cum - ldh)
        w_inv = jnp.exp(-cum)
        w_end = jnp.exp(cum[C - 1:C, :] - cum)
        kkt = kkh * w_ex
        rt = rh * w_in
        kd = kh * w_inv
        bd = bh * w_inv
        a_kk = jnp.where(strict, _mm3(kkt, kd, _NT), 0.0)
        a_kb = jnp.where(strict, _mm3(kkt, bd, _NT), 0.0)
        a_rk = jnp.where(lower, _mm3(rt, kd, _NT), 0.0)
        a_rb = jnp.where(lower, _mm3(rt, bd, _NT), 0.0)
        pw = -a_kb
        tinv = eye + pw
        for _ in range(n_double):
            pw = _mm3(pw, pw)
            tinv = tinv + _mm3(tinv, pw)
        s0 = s_scr[h]
        rhs = _mm3(kkt, s0, _NT) + _mm3(a_kk, vh)
        u = -_mm3(tinv, rhs)
        y = _mm3(rt, s0, _NT) + _mm3(a_rk, vh) + _mm3(a_rb, u)
        s_new = (s0 * w_in[C - 1:C, :] + _mm3(vh, kh * w_end, _TN) + _mm3(u, bh * w_end, _TN))
        s_scr[h] = s_new
        mean = jnp.mean(y, axis=-1, keepdims=True)
        var = jnp.mean(jnp.square(y - mean), axis=-1, keepdims=True)
        yn = (y - mean) * lax.rsqrt(var + GN_EPS) * lnw_ref[:, hs] + lnb_ref[:, hs]
        bonus = jnp.sum(rh * kh * rk_ref[:, hs], axis=-1, keepdims=True) * vh
        y_ref[:, hs] = (yn + bonus) * g[:, hs]

    @pl.when(c == n_chunks - 1)
    def _():
        wkv_ref[...] = s_scr[...]
        shiftout_ref[...] = carry_scr[...]


def rwkv_mix(proj, row0, n_batch, seq, chunk, shift_prev, wkv0, lw):
    n_chunks = seq // chunk
    blk0 = row0 // chunk
    vec = lambda width: pl.BlockSpec((1, width), lambda b, c: (0, 0))
    mat = lambda rows, width: pl.BlockSpec((rows, width), lambda b, c: (0, 0))
    kern = functools.partial(_rwkv_kernel, chunk=chunk, n_chunks=n_chunks)
    return pl.pallas_call(
        kern,
        grid=(n_batch, n_chunks),
        in_specs=[pl.BlockSpec((chunk, RW_PROJ), lambda b, c: (blk0 + b * n_chunks + c, 0)),
                  pl.BlockSpec((None, 1, RW_PROJ), lambda b, c: (b, 0, 0)),
                  pl.BlockSpec((None, RWKV_HEADS, HEAD_DIM, HEAD_DIM), lambda b, c: (b, 0, 0, 0)),
                  vec(RW_PROJ), vec(RWKV_WIDTH), mat(D_DECAY_LORA, RWKV_WIDTH), vec(RWKV_WIDTH),
                  mat(D_AAA_LORA, RWKV_WIDTH), mat(D_GATE_LORA, RWKV_WIDTH),
                  vec(RWKV_WIDTH), vec(RWKV_WIDTH), vec(RWKV_WIDTH), vec(RWKV_WIDTH), vec(RWKV_WIDTH)],
        out_specs=[pl.BlockSpec((chunk, RWKV_WIDTH), lambda b, c: (b * n_chunks + c, 0)),
                   pl.BlockSpec((None, RWKV_HEADS, HEAD_DIM, HEAD_DIM), lambda b, c: (b, 0, 0, 0)),
                   pl.BlockSpec((None, 1, RW_PROJ), lambda b, c: (b, 0, 0))],
        out_shape=[jax.ShapeDtypeStruct((n_batch * seq, RWKV_WIDTH), F32),
                   jax.ShapeDtypeStruct((n_batch, RWKV_HEADS, HEAD_DIM, HEAD_DIM), F32),
                   jax.ShapeDtypeStruct((n_batch, 1, RW_PROJ), F32)],
        scratch_shapes=[pltpu.VMEM((RWKV_HEADS, HEAD_DIM, HEAD_DIM), F32),
                        pltpu.VMEM((1, RW_PROJ), F32)],
        compiler_params=_params("parallel", "arbitrary"),
        name="rwkv_mix",
    )(proj, shift_prev, wkv0, lw['rw_mu'], lw['rw_w0'], lw['rw_w2'], lw['rw_a0'], lw['rw_a2'], lw['rw_g2'],
      lw['rw_k_k'], lw['rw_k_a'], lw['rw_r_k'], lw['rw_ln_w'], lw['rw_ln_b'])


def _nsa_prep_kernel(q_ref, kc_ref, ks_ref, kw_ref, cos_ref, sin_ref, qg_ref, kg_ref, bd_ref,
                     qo_ref, kco_ref, kso_ref, kwo_ref):
    cos = cos_ref[...]
    sin = sin_ref[...]
    half = HEAD_DIM // 2

    def norm_rope(x, gain):
        width = x.shape[1]
        ms = _mm2(x * x, bd_ref[:width, :width])
        y = x * lax.rsqrt(ms + RMS_EPS) * gain
        reps = width // cos.shape[1]
        c = jnp.concatenate([cos] * reps, axis=1) if reps > 1 else cos
        s = jnp.concatenate([sin] * reps, axis=1) if reps > 1 else sin
        lane = lax.broadcasted_iota(jnp.int32, (1, width), 1) % HEAD_DIM
        swapped = jnp.where(lane < half, pltpu.roll(y, width - half, 1), pltpu.roll(y, half, 1))
        return y * c + swapped * s

    qo_ref[...] = norm_rope(q_ref[...], qg_ref[...])
    kw_half = NSA_KV_HEADS * HEAD_DIM
    for br, (src, dst) in enumerate(((kc_ref, kco_ref), (ks_ref, kso_ref), (kw_ref, kwo_ref))):
        kv = src[...]
        dst[:, :kw_half] = norm_rope(kv[:, :kw_half], kg_ref[br:br + 1, :])
        dst[:, kw_half:] = kv[:, kw_half:]


def nsa_prep(proj, cos_tab, sin_tab, q_gain, k_gain, bd, tm, tiles_per_seq, n_prompt_tiles):
    n = proj.shape[0]
    tab_idx = lambda i: (jnp.where(i < n_prompt_tiles, i % tiles_per_seq, tiles_per_seq), 0)
    kv_col0 = RW_PROJ // KV_BRANCH
    q_col = (RW_PROJ + 3 * KV_BRANCH) // NSA_WIDTH
    const = lambda shape: pl.BlockSpec(shape, lambda i: (0, 0))
    return pl.pallas_call(
        _nsa_prep_kernel,
        grid=(n // tm,),
        in_specs=[pl.BlockSpec((tm, NSA_WIDTH), lambda i: (i, q_col)),
                  pl.BlockSpec((tm, KV_BRANCH), lambda i: (i, kv_col0)),
                  pl.BlockSpec((tm, KV_BRANCH), lambda i: (i, kv_col0 + 1)),
                  pl.BlockSpec((tm, KV_BRANCH), lambda i: (i, kv_col0 + 2)),
                  pl.BlockSpec((tm, 2 * HEAD_DIM), tab_idx),
                  pl.BlockSpec((tm, 2 * HEAD_DIM), tab_idx),
                  const((1, NSA_WIDTH)), const((3, 2 * HEAD_DIM)), const((NSA_WIDTH, NSA_WIDTH))],
        out_specs=[pl.BlockSpec((tm, NSA_WIDTH), lambda i: (i, 0))] +
                  [pl.BlockSpec((tm, KV_BRANCH), lambda i: (i, 0))] * 3,
        out_shape=[jax.ShapeDtypeStruct((n, NSA_WIDTH), F32)] +
                  [jax.ShapeDtypeStruct((n, KV_BRANCH), F32)] * 3,
        compiler_params=_params("parallel"),
        name="nsa_prep",
    )(proj, proj, proj, proj, cos_tab, sin_tab, q_gain, k_gain, bd)


def _gelu_tanh(x):
    return 0.5 * x * (1.0 + jnp.tanh(math.sqrt(2.0 / math.pi) * (x + 0.044715 * (x * x * x))))


def _compress_body(srcs_k, srcs_v, pek_ref, pev_ref, w1k_ref, w1v_ref, w2k_ref, w2v_ref, o_ref):
    half = NSA_KV_HEADS * HEAD_DIM
    n_grp = sum(s.shape[0] for s in srcs_k) // CMP_STRIDE
    acc = [jnp.zeros((n_grp, NSA_KV_HEADS * CMP_HIDDEN), F32) for _ in range(4)]

    def every_16th(srcs, l):
        parts = [s[pl.ds(l, s.shape[0] // CMP_STRIDE, stride=CMP_STRIDE), :] for s in srcs]
        return jnp.concatenate(parts, axis=0) if len(parts) > 1 else parts[0]

    for l in range(CMP_STRIDE):
        xk, xv = every_16th(srcs_k, l), every_16th(srcs_v, l)
        l2 = CMP_STRIDE + l
        acc[0] += _mm(xk + pek_ref[l:l + 1, :], w1k_ref[l])
        acc[1] += _mm(xk + pek_ref[l2:l2 + 1, :], w1k_ref[l2])
        acc[2] += _mm(xv + pev_ref[l:l + 1, :], w1v_ref[l])
        acc[3] += _mm(xv + pev_ref[l2:l2 + 1, :], w1v_ref[l2])
    pre_k = acc[0] + pltpu.roll(acc[1], n_grp - 1, 0)
    pre_v = acc[2] + pltpu.roll(acc[3], n_grp - 1, 0)
    o_ref[:, :half] = _mm(_gelu_tanh(pre_k), w2k_ref[...])
    o_ref[:, half:] = _mm(_gelu_tanh(pre_v), w2v_ref[...])


def _compress_prompt_kernel(k_ref, v_ref, *rest):
    _compress_body([k_ref], [v_ref], *rest)


def _compress_paged_kernel(pt_ref, *refs, n_pages):
    _compress_body(list(refs[:n_pages]), list(refs[n_pages:2 * n_pages]), *refs[2 * n_pages:])


def _compress_weight_specs(index_map):
    hid2 = NSA_KV_HEADS * CMP_HIDDEN
    half = NSA_KV_HEADS * HEAD_DIM
    return [pl.BlockSpec((CMP_BLOCK, half), index_map(2)), pl.BlockSpec((CMP_BLOCK, half), index_map(2)),
            pl.BlockSpec((CMP_BLOCK, half, hid2), index_map(3)), pl.BlockSpec((CMP_BLOCK, half, hid2), index_map(3)),
            pl.BlockSpec((hid2, half), index_map(2)), pl.BlockSpec((hid2, half), index_map(2))]


def compress_prompt(kvc, n_batch, seq, cw):
    zeros = lambda nd: (lambda b: (0,) * nd)
    n_grp = seq // CMP_STRIDE
    return pl.pallas_call(
        _compress_prompt_kernel,
        grid=(n_batch,),
        in_specs=[pl.BlockSpec((seq, KV_BRANCH // 2), lambda b: (b, 0)),
                  pl.BlockSpec((seq, KV_BRANCH // 2), lambda b: (b, 1))] + _compress_weight_specs(zeros),
        out_specs=pl.BlockSpec((None, n_grp, KV_BRANCH), lambda b: (b, 0, 0)),
        out_shape=jax.ShapeDtypeStruct((n_batch, n_grp, KV_BRANCH), F32),
        compiler_params=_params("parallel"),
        name="compress_prompt",
    )(kvc, kvc, *cw)


def compress_paged(cache, layer, page_table, cw):
    n_batch, n_pages = page_table.shape
    page = cache.shape[2]
    n_grp = n_pages * page // CMP_STRIDE
    zeros = lambda nd: (lambda b, pt: (0,) * nd)
    page_spec = lambda p, kv: pl.BlockSpec((None, None, page, KV_BRANCH // 2),
                                           lambda b, pt: (layer, pt[b, p], 0, kv))
    grid_spec = pltpu.PrefetchScalarGridSpec(
        num_scalar_prefetch=1,
        grid=(n_batch,),
        in_specs=[page_spec(p, 0) for p in range(n_pages)] + [page_spec(p, 1) for p in range(n_pages)] +
                 _compress_weight_specs(zeros),
        out_specs=pl.BlockSpec((None, n_grp, KV_BRANCH), lambda b, pt: (b, 0, 0)),
    )
    return pl.pallas_call(
        functools.partial(_compress_paged_kernel, n_pages=n_pages),
        grid_spec=grid_spec,
        out_shape=jax.ShapeDtypeStruct((n_batch, n_grp, KV_BRANCH), F32),
        compiler_params=_params("parallel"),
        name="compress_paged",
    )(page_table, *([cache] * (2 * n_pages)), *cw)


def _softmax_step(carry, s, mask, v):
    m, l, acc = carry
    s = jnp.where(mask, s, NEG)
    m_new = jnp.maximum(m, jnp.max(s, axis=-1, keepdims=True))
    alpha = jnp.exp(m - m_new)
    p = jnp.where(mask, jnp.exp(s - m_new), 0.0)
    l = alpha * l + jnp.sum(p, axis=-1, keepdims=True)
    acc = alpha * acc + _mm(p, v)
    return m_new, l, acc


def _softmax_init(rows):
    return (jnp.full((rows, 1), NEG, F32), jnp.zeros((rows, 1), F32), jnp.zeros((rows, HEAD_DIM), F32))


def _softmax_out(carry):
    _, l, acc = carry
    return acc / jnp.where(l > 0.0, l, 1.0)


def _compressed_branch(q, qpos, kc, vc, n_cmp):
    n_grp = kc.shape[0]
    s = _mm3(q, kc, _NT)
    n_idx = lax.broadcasted_iota(jnp.int32, (1, n_grp), 1)
    mask = (n_idx * CMP_STRIDE + (CMP_BLOCK - 1) <= qpos) & (n_idx < n_cmp)
    s = jnp.where(mask, s, NEG)
    e = jnp.where(mask, jnp.exp(s - jnp.max(s, axis=-1, keepdims=True)), 0.0)
    den = jnp.sum(e, axis=-1, keepdims=True)
    p = e / jnp.where(den > 0.0, den, 1.0)
    return _mm(p, vc), p


def _select_blocks(p_sum, qpos, n_cmp, n_slc, width):
    n_grp = p_sum.shape[1]
    n_i = lax.broadcasted_iota(jnp.int32, (n_grp, width), 0) * CMP_STRIDE
    j_i = lax.broadcasted_iota(jnp.int32, (n_grp, width), 1) * SEL_BLOCK
    overlap = (n_i < j_i + SEL_BLOCK) & (n_i + CMP_BLOCK > j_i) & (n_i < n_cmp * CMP_STRIDE)
    imp = _mm2(p_sum, overlap.astype(F32))
    j = lax.broadcasted_iota(jnp.int32, (1, width), 1)
    cur = qpos // SEL_BLOCK
    valid = (j * SEL_BLOCK <= qpos) & (j < n_slc)
    forced = (j == 0) | (j == cur) | (j == cur - 1)
    imp = jnp.where(valid, jnp.where(forced, FORCE_SCORE, imp), NEG)
    rank = jnp.zeros(imp.shape, F32)
    for jp in range(n_slc):
        col = imp[:, jp:jp + 1]
        ahead = (col > imp) | ((col == imp) & (j > jp))
        rank = rank + jnp.where(ahead, 1.0, 0.0)
    return jnp.where((rank < float(min(N_SEL, n_slc))) & valid, 1.0, 0.0)


def _stack_heads(x, g):
    return jnp.concatenate([x[:, (g * NSA_GROUP + r) * HEAD_DIM:(g * NSA_GROUP + r + 1) * HEAD_DIM]
                            for r in range(NSA_GROUP)], axis=0)


def _stack_gate(gsig, g, branch):
    return jnp.concatenate([gsig[:, 3 * (g * NSA_GROUP + r) + branch:3 * (g * NSA_GROUP + r) + branch + 1]
                            for r in range(NSA_GROUP)], axis=0)


def _block_expand(kb2, width, n_keys):
    j = lax.broadcasted_iota(jnp.int32, (width, n_keys), 0)
    c = lax.broadcasted_iota(jnp.int32, (width, n_keys), 1)
    return jnp.where(j == kb2 + c // SEL_BLOCK, 1.0, 0.0)


def _attn_prompt_kernel(q_ref, gate_ref, cmp_ref, ks_ref, kw_ref, o_ref, *, seq, tq):
    i = pl.program_id(1)
    scale = HEAD_DIM ** -0.5
    q_all = q_ref[...] * scale
    gsig = _sigmoid(gate_ref[...])
    n_grp = seq // CMP_STRIDE
    n_cmp = (seq - CMP_BLOCK) // CMP_STRIDE + 1
    n_slc = -(-seq // SEL_BLOCK)
    qpos1 = i * tq + lax.broadcasted_iota(jnp.int32, (tq, 1), 0)
    qpos = jnp.concatenate([qpos1] * NSA_GROUP, axis=0)
    rows = NSA_GROUP * tq
    half = NSA_KV_HEADS * HEAD_DIM
    win_blocks = WINDOW // KEY_BLOCK
    for g in range(NSA_KV_HEADS):
        gs = slice(g * HEAD_DIM, (g + 1) * HEAD_DIM)
        vs = slice(half + g * HEAD_DIM, half + (g + 1) * HEAD_DIM)
        q = _stack_heads(q_all, g)
        o_c, p_c = _compressed_branch(q, qpos, cmp_ref[:, gs], cmp_ref[:, vs], n_cmp)
        p_sum = p_c[0:tq]
        for r in range(1, NSA_GROUP):
            p_sum = p_sum + p_c[r * tq:(r + 1) * tq]
        sel = _select_blocks(p_sum, qpos1, n_cmp, n_slc, n_slc)
        sel4 = jnp.concatenate([sel] * NSA_GROUP, axis=0)

        def sel_body(kb, carry):
            start = pl.multiple_of(kb * KEY_BLOCK, KEY_BLOCK)
            kblk = ks_ref[pl.ds(start, KEY_BLOCK), gs]
            vblk = ks_ref[pl.ds(start, KEY_BLOCK), vs]
            s = _mm(q, kblk, _NT)
            kpos = start + lax.broadcasted_iota(jnp.int32, (1, KEY_BLOCK), 1)
            picked = _mm(sel4, _block_expand(kb * (KEY_BLOCK // SEL_BLOCK), n_slc, KEY_BLOCK))
            mask = (kpos <= qpos) & (picked > 0.5)
            return _softmax_step(carry, s, mask, vblk)

        o_s = _softmax_out(lax.fori_loop(0, i * (tq // KEY_BLOCK) + tq // KEY_BLOCK, sel_body, _softmax_init(rows)))

        def win_body(kb, carry):
            start = pl.multiple_of(kb * KEY_BLOCK, KEY_BLOCK)
            kblk = kw_ref[pl.ds(start, KEY_BLOCK), gs]
            vblk = kw_ref[pl.ds(start, KEY_BLOCK), vs]
            s = _mm(q, kblk, _NT)
            dpos = qpos - (start + lax.broadcasted_iota(jnp.int32, (1, KEY_BLOCK), 1))
            mask = (dpos >= 0) & (dpos < WINDOW)
            return _softmax_step(carry, s, mask, vblk)

        first = jnp.maximum(i * (tq // KEY_BLOCK) - win_blocks, 0)
        o_w = _softmax_out(lax.fori_loop(first, i * (tq // KEY_BLOCK) + tq // KEY_BLOCK, win_body,
                                         _softmax_init(rows)))
        o = (o_c * _stack_gate(gsig, g, 0) + o_s * _stack_gate(gsig, g, 1) + o_w * _stack_gate(gsig, g, 2))
        for r in range(NSA_GROUP):
            h = g * NSA_GROUP + r
            o_ref[:, h * HEAD_DIM:(h + 1) * HEAD_DIM] = o[r * tq:(r + 1) * tq]


def attn_prompt(q_rot, proj, cmp_kv, kvs, kvw, n_batch, seq, tq):
    n_qb = seq // tq
    gate_col = (RW_PROJ + 3 * KV_BRANCH + NSA_WIDTH) // GATE_PAD
    n_grp = seq // CMP_STRIDE
    return pl.pallas_call(
        functools.partial(_attn_prompt_kernel, seq=seq, tq=tq),
        grid=(n_batch, n_qb),
        in_specs=[pl.BlockSpec((tq, NSA_WIDTH), lambda b, i: (b * n_qb + i, 0)),
                  pl.BlockSpec((tq, GATE_PAD), lambda b, i: (b * n_qb + i, gate_col)),
                  pl.BlockSpec((None, n_grp, KV_BRANCH), lambda b, i: (b, 0, 0)),
                  pl.BlockSpec((seq, KV_BRANCH), lambda b, i: (b, 0)),
                  pl.BlockSpec((seq, KV_BRANCH), lambda b, i: (b, 0))],
        out_specs=pl.BlockSpec((tq, NSA_WIDTH), lambda b, i: (b * n_qb + i, 0)),
        out_shape=jax.ShapeDtypeStruct((n_batch * seq, NSA_WIDTH), F32),
        compiler_params=_params("parallel", "arbitrary"),
        name="attn_prompt",
    )(q_rot, proj, cmp_kv, kvs, kvw)


def _pad_rows(x, rows):
    return jnp.concatenate([x, jnp.zeros((rows - x.shape[0], x.shape[1]), x.dtype)], axis=0)


def _attn_sample_kernel(pt_ref, q_ref, gate_ref, cmp_ref, *refs, n_pages, past, win_buf):
    pages = refs[:n_pages]
    snew_ref, win_ref, wnew_ref, o_ref, winout_ref = refs[n_pages:]
    tq = q_ref.shape[0]
    page = pages[0].shape[0]
    scale = HEAD_DIM ** -0.5
    q_all = q_ref[...] * scale
    gsig = _sigmoid(gate_ref[...])
    seq = past + tq
    n_cmp = (seq - CMP_BLOCK) // CMP_STRIDE + 1
    n_slc = -(-seq // SEL_BLOCK)
    width = 128 * (-(-n_slc // 128))
    qpos1 = past + lax.broadcasted_iota(jnp.int32, (tq, 1), 0)
    qpos = jnp.concatenate([qpos1] * NSA_GROUP, axis=0)
    rows = NSA_GROUP * tq
    half = NSA_KV_HEADS * HEAD_DIM
    key_i = lax.broadcasted_iota(jnp.int32, (1, KEY_BLOCK), 1)
    snew = _pad_rows(snew_ref[...], KEY_BLOCK)
    wnew = _pad_rows(wnew_ref[...], KEY_BLOCK)
    for g in range(NSA_KV_HEADS):
        gs = slice(g * HEAD_DIM, (g + 1) * HEAD_DIM)
        vs = slice(half + g * HEAD_DIM, half + (g + 1) * HEAD_DIM)
        q = _stack_heads(q_all, g)
        o_c, p_c = _compressed_branch(q, qpos, cmp_ref[:, gs], cmp_ref[:, vs], n_cmp)
        p_sum = p_c[0:tq]
        for r in range(1, NSA_GROUP):
            p_sum = p_sum + p_c[r * tq:(r + 1) * tq]
        sel = _select_blocks(p_sum, qpos1, n_cmp, n_slc, width)
        sel4 = jnp.concatenate([sel] * NSA_GROUP, axis=0)
        carry = _softmax_init(rows)
        for p in range(n_pages):
            for kb in range(page // KEY_BLOCK):
                start = p * page + kb * KEY_BLOCK
                kblk = pages[p][kb * KEY_BLOCK:(kb + 1) * KEY_BLOCK, gs]
                vblk = pages[p][kb * KEY_BLOCK:(kb + 1) * KEY_BLOCK, vs]
                picked = _mm(sel4, _block_expand(start // SEL_BLOCK, width, KEY_BLOCK))
                mask = (start + key_i <= qpos) & (picked > 0.5)
                carry = _softmax_step(carry, _mm(q, kblk, _NT), mask, vblk)
        picked = _mm(sel4, _block_expand(past // SEL_BLOCK, width, KEY_BLOCK))
        mask = (past + key_i <= qpos) & (picked > 0.5) & (key_i < tq)
        carry = _softmax_step(carry, _mm(q, snew[:, gs], _NT), mask, snew[:, vs])
        o_s = _softmax_out(carry)
        carry = _softmax_init(rows)
        for kb in range(win_buf // KEY_BLOCK):
            kpos = past - win_buf + kb * KEY_BLOCK + key_i
            dpos = qpos - kpos
            mask = (kpos >= 0) & (dpos >= 0) & (dpos < WINDOW)
            carry = _softmax_step(carry, _mm(q, win_ref[kb * KEY_BLOCK:(kb + 1) * KEY_BLOCK, gs], _NT), mask,
                                  win_ref[kb * KEY_BLOCK:(kb + 1) * KEY_BLOCK, vs])
        dpos = qpos - (past + key_i)
        mask = (dpos >= 0) & (dpos < WINDOW) & (key_i < tq)
        carry = _softmax_step(carry, _mm(q, wnew[:, gs], _NT), mask, wnew[:, vs])
        o_w = _softmax_out(carry)
        o = (o_c * _stack_gate(gsig, g, 0) + o_s * _stack_gate(gsig, g, 1) + o_w * _stack_gate(gsig, g, 2))
        for r in range(NSA_GROUP):
            h = g * NSA_GROUP + r
            o_ref[:, h * HEAD_DIM:(h + 1) * HEAD_DIM] = o[r * tq:(r + 1) * tq]
    winout_ref[0:win_buf - tq, :] = win_ref[tq:win_buf, :]
    winout_ref[win_buf - tq:win_buf, :] = wnew_ref[...]


def attn_sample(q_rot, proj, cmp_kv, cache_sel, layer, page_table, kvs, win_state, kvw, row0, tq):
    n_batch, n_pages = page_table.shape
    page = cache_sel.shape[2]
    past = n_pages * page
    win_buf = win_state.shape[2]
    blk0 = row0 // tq
    gate_col = (RW_PROJ + 3 * KV_BRANCH + NSA_WIDTH) // GATE_PAD
    n_grp = cmp_kv.shape[1]
    page_spec = lambda p: pl.BlockSpec((None, None, page, KV_BRANCH), lambda b, pt: (layer, pt[b, p], 0, 0))
    grid_spec = pltpu.PrefetchScalarGridSpec(
        num_scalar_prefetch=1,
        grid=(n_batch,),
        in_specs=[pl.BlockSpec((tq, NSA_WIDTH), lambda b, pt: (blk0 + b, 0)),
                  pl.BlockSpec((tq, GATE_PAD), lambda b, pt: (blk0 + b, gate_col)),
                  pl.BlockSpec((None, n_grp, KV_BRANCH), lambda b, pt: (b, 0, 0))] +
                 [page_spec(p) for p in range(n_pages)] +
                 [pl.BlockSpec((tq, KV_BRANCH), lambda b, pt: (blk0 + b, 0)),
                  pl.BlockSpec((None, None, win_buf, KV_BRANCH), lambda b, pt: (layer, b, 0, 0)),
                  pl.BlockSpec((tq, KV_BRANCH), lambda b, pt: (blk0 + b, 0))],
        out_specs=[pl.BlockSpec((tq, NSA_WIDTH), lambda b, pt: (b, 0)),
                   pl.BlockSpec((None, win_buf, KV_BRANCH), lambda b, pt: (b, 0, 0))],
    )
    return pl.pallas_call(
        functools.partial(_attn_sample_kernel, n_pages=n_pages, past=past, win_buf=win_buf),
        grid_spec=grid_spec,
        out_shape=[jax.ShapeDtypeStruct((n_batch * tq, NSA_WIDTH), F32),
                   jax.ShapeDtypeStruct((n_batch, win_buf, KV_BRANCH), F32)],
        compiler_params=_params("parallel"),
        name="attn_sample",
    )(page_table, q_rot, proj, cmp_kv, *([cache_sel] * n_pages), kvs, win_state, kvw)


def _out_proj_kernel(x_ref, yp_ref, ys_ref, op_ref, os_ref, g_ref, wa_ref, wb_ref, out_ref, *, n_prompt_tiles):
    is_prompt = pl.program_id(0) < n_prompt_tiles
    y = jnp.where(is_prompt, yp_ref[...], ys_ref[...])
    o = _rms(jnp.where(is_prompt, op_ref[...], os_ref[...]), g_ref[...])
    out_ref[...] = x_ref[...] + _dg(y.astype(BF16), wa_ref[...], _NN) + _dg(o.astype(BF16), wb_ref[...], _NN)


def out_proj(x, y_p, y_s, o_p, o_s, gain, w_a, w_b, tm):
    n, d = x.shape
    npt = y_p.shape[0] // tm
    p_idx = lambda i: (jnp.minimum(i, npt - 1), 0)
    s_idx = lambda i: (jnp.maximum(i - npt, 0), 0)
    const = lambda shape: pl.BlockSpec(shape, lambda i: (0, 0))
    return pl.pallas_call(
        functools.partial(_out_proj_kernel, n_prompt_tiles=npt),
        grid=(n // tm,),
        in_specs=[pl.BlockSpec((tm, d), lambda i: (i, 0)),
                  pl.BlockSpec((tm, RWKV_WIDTH), p_idx), pl.BlockSpec((tm, RWKV_WIDTH), s_idx),
                  pl.BlockSpec((tm, NSA_WIDTH), p_idx), pl.BlockSpec((tm, NSA_WIDTH), s_idx),
                  const((1, NSA_WIDTH)), const((RWKV_WIDTH, d)), const((NSA_WIDTH, d))],
        out_specs=pl.BlockSpec((tm, d), lambda i: (i, 0)),
        out_shape=jax.ShapeDtypeStruct((n, d), F32),
        compiler_params=_params("parallel"),
        name="out_proj",
    )(x, y_p, y_s, o_p, o_s, gain, w_a, w_b)


def _swiglu_partial(h, wg, wu, wd):
    a = _dg(h, wg, _NN)
    u = _dg(h, wu, _NN)
    return _dg((a * _sigmoid(a) * u).astype(BF16), wd, _NN)


def _ffn_kernel(x_ref, g_ref, wg_ref, wu_ref, wd_ref, o_ref, h_scr, acc_scr, *, n_f):
    f = pl.program_id(1)

    @pl.when(f == 0)
    def _():
        x = x_ref[...]
        h_scr[...] = _rms(x, g_ref[...]).astype(BF16)
        acc_scr[...] = x

    acc_scr[...] += _swiglu_partial(h_scr[...], wg_ref[...], wu_ref[...], wd_ref[...])

    @pl.when(f == n_f - 1)
    def _():
        o_ref[...] = acc_scr[...]


def ffn(x, gain, wg, wu, wd, tm, tf):
    n, d = x.shape
    n_f = wg.shape[1] // tf
    return pl.pallas_call(
        functools.partial(_ffn_kernel, n_f=n_f),
        grid=(n // tm, n_f),
        in_specs=[pl.BlockSpec((tm, d), lambda i, f: (i, 0)),
                  pl.BlockSpec((1, d), lambda i, f: (0, 0)),
                  pl.BlockSpec((d, tf), lambda i, f: (0, f)),
                  pl.BlockSpec((d, tf), lambda i, f: (0, f)),
                  pl.BlockSpec((tf, d), lambda i, f: (f, 0))],
        out_specs=pl.BlockSpec((tm, d), lambda i, f: (i, 0)),
        out_shape=jax.ShapeDtypeStruct((n, d), F32),
        scratch_shapes=[pltpu.VMEM((tm, d), BF16), pltpu.VMEM((tm, d), F32)],
        compiler_params=_params("parallel", "arbitrary"),
        name="ffn",
    )(x, gain, wg, wu, wd)


def _route_kernel(x_ref, g_ref, wr_ref, br_ref, o_ref):
    h = _rms(x_ref[...], g_ref[...])
    logits = _mm3(h, wr_ref[...]) + br_ref[...]
    lane = lax.broadcasted_iota(jnp.int32, logits.shape, 1)
    low = -3.0e38
    lg = jnp.where(lane < N_EXPERTS, logits, low)
    m1 = jnp.max(lg, axis=-1, keepdims=True)
    i1 = jnp.min(jnp.where(lg == m1, lane, lg.shape[1]), axis=-1, keepdims=True)
    lg2 = jnp.where(lane == i1, low, lg)
    m2 = jnp.max(lg2, axis=-1, keepdims=True)
    i2 = jnp.min(jnp.where(lg2 == m2, lane, lg.shape[1]), axis=-1, keepdims=True)
    e2 = jnp.exp(m2 - m1)
    den = 1.0 + e2
    o_ref[...] = jnp.where(lane == i1, 1.0 / den, 0.0) + jnp.where(lane == i2, e2 / den, 0.0)


def moe_route(x, gain, w_router, b_router, tm):
    n, d = x.shape
    pad = w_router.shape[1]
    return pl.pallas_call(
        _route_kernel,
        grid=(n // tm,),
        in_specs=[pl.BlockSpec((tm, d), lambda i: (i, 0)),
                  pl.BlockSpec((1, d), lambda i: (0, 0)),
                  pl.BlockSpec((d, pad), lambda i: (0, 0)),
                  pl.BlockSpec((1, pad), lambda i: (0, 0))],
        out_specs=pl.BlockSpec((tm, pad), lambda i: (i, 0)),
        out_shape=jax.ShapeDtypeStruct((n, pad), F32),
        compiler_params=_params("parallel"),
        name="moe_route",
    )(x, gain, w_router, b_router)


def _moe_kernel(x_ref, g_ref, gate_ref, wg_ref, wu_ref, wd_ref, o_ref, h_scr, acc_scr, *, n_e, n_f):
    e = pl.program_id(1)
    f = pl.program_id(2)

    @pl.when((e == 0) & (f == 0))
    def _():
        x = x_ref[...]
        h_scr[...] = _rms(x, g_ref[...]).astype(BF16)
        acc_scr[...] = x

    gates = gate_ref[...]
    lane = lax.broadcasted_iota(jnp.int32, gates.shape, 1)
    gate_e = jnp.sum(jnp.where(lane == e, gates, 0.0), axis=-1, keepdims=True)
    acc_scr[...] += _swiglu_partial(h_scr[...], wg_ref[...], wu_ref[...], wd_ref[...]) * gate_e

    @pl.when((e == n_e - 1) & (f == n_f - 1))
    def _():
        o_ref[...] = acc_scr[...]


def moe(x, gain, gates, wg, wu, wd, tm, tf):
    n, d = x.shape
    n_e = wg.shape[0]
    n_f = wg.shape[2] // tf
    return pl.pallas_call(
        functools.partial(_moe_kernel, n_e=n_e, n_f=n_f),
        grid=(n // tm, n_e, n_f),
        in_specs=[pl.BlockSpec((tm, d), lambda i, e, f: (i, 0)),
                  pl.BlockSpec((1, d), lambda i, e, f: (0, 0)),
                  pl.BlockSpec((tm, gates.shape[1]), lambda i, e, f: (i, 0)),
                  pl.BlockSpec((None, d, tf), lambda i, e, f: (e, 0, f)),
                  pl.BlockSpec((None, d, tf), lambda i, e, f: (e, 0, f)),
                  pl.BlockSpec((None, tf, d), lambda i, e, f: (e, f, 0))],
        out_specs=pl.BlockSpec((tm, d), lambda i, e, f: (i, 0)),
        out_shape=jax.ShapeDtypeStruct((n, d), F32),
        scratch_shapes=[pltpu.VMEM((tm, d), BF16), pltpu.VMEM((tm, d), F32)],
        compiler_params=_params("parallel", "arbitrary", "arbitrary"),
        name="moe",
    )(x, gain, gates, wg, wu, wd)


def _rope_tables(pos):
    half = HEAD_DIM // 2
    inv = ROPE_THETA ** (-jnp.arange(half, dtype=F32) / half)
    ang = pos.astype(F32)[:, None] * inv[None, :]
    cos, sin = jnp.cos(ang), jnp.sin(ang)
    return jnp.tile(jnp.concatenate([cos, cos], axis=1), (1, 2)), jnp.tile(jnp.concatenate([-sin, sin], axis=1), (1, 2))


def _block_diag2(w):
    z = jnp.zeros_like(w)
    return jnp.concatenate([jnp.concatenate([w, z], axis=-1), jnp.concatenate([z, w], axis=-1)], axis=-2)


def kernel(x_prompt, x_sample, cache_cmp_kv, cache_sel_kv, state_win_kv, state_wkv, state_shift, page_table,
           g_attn, w_in, w_out, rw_mu, rw_w0, rw_w2, rw_a0, rw_a2, rw_g2, rw_k_k, rw_k_a, rw_r_k, rw_ln_w, rw_ln_b,
           nsa_q_gain, nsa_k_gain, cmp_pe_k, cmp_w1_k, cmp_w2_k, cmp_pe_v, cmp_w1_v, cmp_w2_v, nsa_out_gain,
           g_ffn, ffn_w_gate, ffn_w_up, ffn_w_down, moe_router, moe_b_router, moe_w_gate, moe_w_up, moe_w_down):
    bp, tp, d = x_prompt.shape
    bs, ts, _ = x_sample.shape
    depth = w_in.shape[0]
    n_p, n_s = bp * tp, bs * ts
    n_phys, page = cache_cmp_kv.shape[1], cache_cmp_kv.shape[2]
    past = page_table.shape[1] * page
    win_buf = state_win_kv.shape[2]
    tm = 512
    x = jnp.concatenate([x_prompt.reshape(n_p, d), x_sample.reshape(n_s, d)], axis=0)
    cache_c = cache_cmp_kv.reshape(depth, n_phys, page, KV_BRANCH)
    cache_s = cache_sel_kv.reshape(depth, n_phys, page, KV_BRANCH)
    win_state = state_win_kv.reshape(depth, bs, win_buf, KV_BRANCH)
    pos = jnp.concatenate([jnp.arange(tp, dtype=jnp.int32), past + jnp.arange(tm, dtype=jnp.int32) % ts])
    cos_tab, sin_tab = _rope_tables(pos)
    head_mean = jnp.kron(jnp.eye(NSA_HEADS, dtype=F32), jnp.full((HEAD_DIM, HEAD_DIM), 1.0 / HEAD_DIM, F32))
    shift0 = jnp.zeros((bp, 1, RW_PROJ), F32)
    wkv0 = jnp.zeros((bp, RWKV_HEADS, HEAD_DIM, HEAD_DIM), F32)
    o_rw, o_kv, o_q = RW_PROJ, RW_PROJ + NSA_WIDTH, RW_PROJ + NSA_WIDTH + 3 * KV_BRANCH
    row = lambda v: v.reshape(1, -1)
    st_p = ([], [], [], [], [])
    st_s = ([], [], [], [], [])
    for l in range(depth):
        wl = w_in[l]
        w_in_l = jnp.concatenate([wl[:, :o_rw], wl[:, o_kv:o_q], wl[:, o_rw:o_kv], wl[:, o_q:],
                                  jnp.zeros((d, GATE_PAD - N_GATE), F32)], axis=1).astype(BF16)
        lw = {'rw_mu': row(rw_mu[l]), 'rw_w0': row(rw_w0[l]), 'rw_w2': rw_w2[l], 'rw_a0': row(rw_a0[l]),
              'rw_a2': rw_a2[l], 'rw_g2': rw_g2[l], 'rw_k_k': row(rw_k_k[l]), 'rw_k_a': row(rw_k_a[l]),
              'rw_r_k': row(rw_r_k[l]), 'rw_ln_w': row(rw_ln_w[l]), 'rw_ln_b': row(rw_ln_b[l])}
        proj = norm_matmul(x, row(g_attn[l]), w_in_l, tm)
        y_p, wkv_p, shift_p = rwkv_mix(proj, 0, bp, tp, RWKV_CHUNK, shift0, wkv0, lw)
        y_s, wkv_s, shift_s = rwkv_mix(proj, n_p, bs, ts, ts, state_shift[l][:, None, :], state_wkv[l], lw)
        q_rot, kvc, kvs, kvw = nsa_prep(proj, cos_tab, sin_tab, jnp.tile(row(nsa_q_gain[l]), (1, NSA_HEADS)),
                                        jnp.tile(nsa_k_gain[l], (1, NSA_KV_HEADS)), head_mean, tm, tp // tm, n_p // tm)
        cw = (jnp.tile(cmp_pe_k[l], (1, NSA_KV_HEADS)), jnp.tile(cmp_pe_v[l], (1, NSA_KV_HEADS)),
              _block_diag2(cmp_w1_k[l].reshape(CMP_BLOCK, HEAD_DIM, CMP_HIDDEN)).astype(BF16),
              _block_diag2(cmp_w1_v[l].reshape(CMP_BLOCK, HEAD_DIM, CMP_HIDDEN)).astype(BF16),
              _block_diag2(cmp_w2_k[l]).astype(BF16), _block_diag2(cmp_w2_v[l]).astype(BF16))
        cmp_p = compress_prompt(kvc, bp, tp, cw)
        cmp_s = compress_paged(cache_c, l, page_table, cw)
        o_p = attn_prompt(q_rot, proj, cmp_p, kvs, kvw, bp, tp, KEY_BLOCK)
        o_s, win_s = attn_sample(q_rot, proj, cmp_s, cache_s, l, page_table, kvs, win_state, kvw, n_p, ts)
        wo = w_out[l].astype(BF16)
        x = out_proj(x, y_p, y_s, o_p, o_s, row(nsa_out_gain[l]), wo[:RWKV_WIDTH], wo[RWKV_WIDTH:], tm)
        i = l // 2
        if l % 2 == 0:
            x = ffn(x, row(g_ffn[l]), ffn_w_gate[i].astype(BF16), ffn_w_up[i].astype(BF16),
                    ffn_w_down[i].astype(BF16), tm, ffn_w_gate.shape[2] // 2)
        else:
            pad = GATE_PAD - N_EXPERTS
            gates = moe_route(x, row(g_ffn[l]), jnp.pad(moe_router[i], ((0, 0), (0, pad))),
                              jnp.pad(row(moe_b_router[i]), ((0, 0), (0, pad))), tm)
            x = moe(x, row(g_ffn[l]), gates, moe_w_gate[i].astype(BF16), moe_w_up[i].astype(BF16),
                    moe_w_down[i].astype(BF16), tm, moe_w_gate.shape[3] // 2)
        kv5 = lambda a, b, t: a.reshape(b, t, 2, NSA_KV_HEADS, HEAD_DIM)
        new_p = (kv5(kvc[:n_p], bp, tp), kv5(kvs[:n_p], bp, tp),
                 kv5(kvw[:n_p], bp, tp)[:, -min(WINDOW, tp):], wkv_p, shift_p[:, 0])
        new_s = (kv5(kvc[n_p:], bs, ts), kv5(kvs[n_p:], bs, ts), kv5(win_s, bs, win_buf), wkv_s, shift_s[:, 0])
        for buf, val in zip(st_p, new_p):
            buf.append(val)
        for buf, val in zip(st_s, new_s):
            buf.append(val)
    outs_p = [jnp.stack(b, axis=0) for b in st_p]
    outs_s = [jnp.stack(b, axis=0) for b in st_s]
    return (x[:n_p].reshape(bp, tp, d), x[n_p:].reshape(bs, ts, d), *outs_p, *outs_s)
```

```python
import functools
import math

import jax
import jax.numpy as jnp
import numpy as np
from jax import lax
from jax.experimental import pallas as pl
from jax.experimental.pallas import tpu as pltpu

F32 = jnp.float32
BF16 = jnp.bfloat16

HEAD_DIM = 64
RWKV_WIDTH = 512
NSA_WIDTH = 512
RWKV_HEADS = RWKV_WIDTH // HEAD_DIM
NSA_HEADS = NSA_WIDTH // HEAD_DIM
NSA_KV_HEADS = 2
NSA_GROUP = NSA_HEADS // NSA_KV_HEADS
D_DECAY_LORA = 64
D_AAA_LORA = 64
D_GATE_LORA = 128
RW_PROJ = 3 * RWKV_WIDTH + D_DECAY_LORA + D_AAA_LORA + D_GATE_LORA
KV_BRANCH = 2 * NSA_KV_HEADS * HEAD_DIM
N_GATE = 3 * NSA_HEADS
GATE_PAD = 128
PROJ_PAD = RW_PROJ + 3 * KV_BRANCH + NSA_WIDTH + GATE_PAD
CMP_BLOCK = 32
CMP_STRIDE = 16
CMP_HIDDEN = 128
SEL_BLOCK = 64
N_SEL = 16
WINDOW = 512
ROPE_THETA = 10000.0
N_EXPERTS = 8
RMS_EPS = 1e-6
GN_EPS = 64e-5
NEG = -1e30
FORCE_SCORE = 1e4
RWKV_CHUNK = 64
KEY_BLOCK = 128

VMEM_LIMIT_BYTES = 56 * 1024 * 1024

_NN = (((1,), (0,)), ((), ()))
_NT = (((1,), (1,)), ((), ()))
_TN = (((0,), (0,)), ((), ()))


def _dg(a, b, dn):
    return lax.dot_general(a, b, dn, preferred_element_type=F32)


def _mm(a, b, dn=_NN):
    return _dg(a.astype(BF16), b.astype(BF16), dn)


def _split(x):
    hi = x.astype(BF16)
    lo = (x - hi.astype(F32)).astype(BF16)
    return hi, lo


def _mm3(a, b, dn=_NN):
    ah, al = _split(a)
    bh, bl = _split(b)
    return _dg(ah, bh, dn) + (_dg(al, bh, dn) + _dg(ah, bl, dn))


def _mm2(a, b_exact, dn=_NN):
    ah, al = _split(a)
    bb = b_exact.astype(BF16)
    return _dg(ah, bb, dn) + _dg(al, bb, dn)


def _mm2r(a_exact, b, dn=_NN):
    bh, bl = _split(b)
    aa = a_exact.astype(BF16)
    return _dg(aa, bh, dn) + _dg(aa, bl, dn)


def _params(*sem):
    return pltpu.CompilerParams(dimension_semantics=sem, vmem_limit_bytes=VMEM_LIMIT_BYTES)


def _sigmoid(x):
    return 1.0 / (1.0 + jnp.exp(-x))


def _rms(x, g):
    return x * lax.rsqrt(jnp.mean(x * x, axis=-1, keepdims=True) + RMS_EPS) * g


def _norm_matmul_kernel(x_ref, g_ref, w_ref, o_ref):
    h = _rms(x_ref[...], g_ref[...])
    o_ref[...] = _dg(h.astype(BF16), w_ref[...], _NN)


def norm_matmul(x, g, w, tm):
    n, d = x.shape
    dout = w.shape[1]
    return pl.pallas_call(
        _norm_matmul_kernel,
        grid=(n // tm,),
        in_specs=[pl.BlockSpec((tm, d), lambda i: (i, 0)),
                  pl.BlockSpec((1, d), lambda i: (0, 0)),
                  pl.BlockSpec((d, dout), lambda i: (0, 0))],
        out_specs=pl.BlockSpec((tm, dout), lambda i: (i, 0)),
        out_shape=jax.ShapeDtypeStruct((n, dout), F32),
        compiler_params=_params("parallel"),
        name="norm_matmul",
    )(x, g, w)


def _rwkv_kernel(p_ref, shift_ref, wkv0_ref, mu_ref, w0_ref, w2_ref, a0_ref, a2_ref, g2_ref,
                 kk_ref, ka_ref, rk_ref, lnw_ref, lnb_ref,
                 y_ref, wkv_ref, shiftout_ref, s_scr, carry_scr, *, chunk, n_chunks):
    c = pl.program_id(1)
    C, N, H = chunk, HEAD_DIM, RWKV_HEADS

    @pl.when(c == 0)
    def _():
        s_scr[...] = wkv0_ref[...]
        carry_scr[...] = shift_ref[...]

    p = p_ref[...]
    row = lax.broadcasted_iota(jnp.int32, (C, 1), 0)
    p_prev = jnp.where(row == 0, carry_scr[...], pltpu.roll(p, 1, 0))
    carry_scr[...] = p[C - 1:C, :]
    xs = p + (p_prev - p) * mu_ref[...]
    o = 3 * RWKV_WIDTH
    r = xs[:, :RWKV_WIDTH]
    k = xs[:, RWKV_WIDTH:2 * RWKV_WIDTH]
    v = xs[:, 2 * RWKV_WIDTH:o]
    w_lr = xs[:, o:o + D_DECAY_LORA]
    a_lr = xs[:, o + D_DECAY_LORA:o + D_DECAY_LORA + D_AAA_LORA]
    g_lr = xs[:, o + D_DECAY_LORA + D_AAA_LORA:]
    z = -(w0_ref[...] + _mm3(jnp.tanh(w_lr), w2_ref[...]))
    softplus = jnp.maximum(z, 0.0) + jnp.log(1.0 + jnp.exp(-jnp.abs(z)))
    ld = -jnp.exp(-softplus - 0.5)
    a = _sigmoid(a0_ref[...] + _mm3(a_lr, a2_ref[...]))
    g = _mm3(_sigmoid(g_lr), g2_ref[...])
    kk_all = k * kk_ref[...]
    k_all = k * (1.0 + (a - 1.0) * ka_ref[...])

    ti = lax.broadcasted_iota(jnp.int32, (C, C), 0)
    si = lax.broadcasted_iota(jnp.int32, (C, C), 1)
    lower = (si <= ti)
    strict = (si < ti)
    tril = lower.astype(F32)
    eye = (si == ti).astype(F32)
    n_double = max(int(math.ceil(math.log2(C))) - 1, 0)

    heads = range(H)
    sl = [slice(h * N, (h + 1) * N) for h in heads]
    rh = [r[:, s] for s in sl]
    kh = [k_all[:, s] for s in sl]
    vh = [v[:, s] for s in sl]
    kkh = [kk_all[:, s] for s in sl]
    kkh = [x / jnp.maximum(jnp.sqrt(jnp.sum(x * x, axis=-1, keepdims=True)), 1e-12) for x in kkh]
    bh = [kkh[h] * a[:, sl[h]] for h in heads]
    ldh = [ld[:, s] for s in sl]
    cum = [_mm2r(tril, ldh[h]) for h in heads]
    w_in = [jnp.exp(x) for x in cum]
    w_inv = [jnp.exp(-x) for x in cum]
    w_end = [jnp.exp(x[C - 1:C, :] - x) for x in cum]
    lhs = [jnp.concatenate([kkh[h] * jnp.exp(cum[h] - ldh[h]), rh[h] * w_in[h]], axis=0).astype(BF16)
           for h in heads]
    kd = [(kh[h] * w_inv[h]).astype(BF16) for h in heads]
    bd = [(bh[h] * w_inv[h]).astype(BF16) for h in heads]
    g_k = [_dg(lhs[h], kd[h], _NT) for h in heads]
    g_b = [_dg(lhs[h], bd[h], _NT) for h in heads]
    a_kk = [jnp.where(strict, x[:C], 0.0) for x in g_k]
    a_rk = [jnp.where(lower, x[C:], 0.0) for x in g_k]
    a_rb = [jnp.where(lower, x[C:], 0.0) for x in g_b]
    pw = [jnp.where(strict, -x[:C], 0.0) for x in g_b]
    tinv = [eye + x for x in pw]
    for _ in range(n_double):
        pw = [_mm(x, x) for x in pw]
        tinv = [tinv[h] + _mm(tinv[h], pw[h]) for h in heads]
    s0 = [s_scr[h] for h in heads]
    g_s = [_mm(lhs[h], s0[h], _NT) for h in heads]
    rhs = [g_s[h][:C] + _mm(a_kk[h], vh[h]) for h in heads]
    u = [-_mm(tinv[h], rhs[h]) for h in heads]
    y = [g_s[h][C:] + _mm(a_rk[h], vh[h]) + _mm(a_rb[h], u[h]) for h in heads]
    for h in heads:
        s_scr[h] = (s0[h] * w_in[h][C - 1:C, :] + _mm(vh[h], kh[h] * w_end[h], _TN)
                    + _mm(u[h], bh[h] * w_end[h], _TN))
    for h in heads:
        mean = jnp.mean(y[h], axis=-1, keepdims=True)
        var = jnp.mean(jnp.square(y[h] - mean), axis=-1, keepdims=True)
        yn = (y[h] - mean) * lax.rsqrt(var + GN_EPS) * lnw_ref[:, sl[h]] + lnb_ref[:, sl[h]]
        bonus = jnp.sum(rh[h] * kh[h] * rk_ref[:, sl[h]], axis=-1, keepdims=True) * vh[h]
        y_ref[:, sl[h]] = (yn + bonus) * g[:, sl[h]]

    @pl.when(c == n_chunks - 1)
    def _():
        wkv_ref[...] = s_scr[...]
        shiftout_ref[...] = carry_scr[...]


def rwkv_mix(proj, row0, n_batch, seq, chunk, shift_prev, wkv0, lw):
    n_chunks = seq // chunk
    blk0 = row0 // chunk
    vec = lambda width: pl.BlockSpec((1, width), lambda b, c: (0, 0))
    mat = lambda rows, width: pl.BlockSpec((rows, width), lambda b, c: (0, 0))
    kern = functools.partial(_rwkv_kernel, chunk=chunk, n_chunks=n_chunks)
    return pl.pallas_call(
        kern,
        grid=(n_batch, n_chunks),
        in_specs=[pl.BlockSpec((chunk, RW_PROJ), lambda b, c: (blk0 + b * n_chunks + c, 0)),
                  pl.BlockSpec((None, 1, RW_PROJ), lambda b, c: (b, 0, 0)),
                  pl.BlockSpec((None, RWKV_HEADS, HEAD_DIM, HEAD_DIM), lambda b, c: (b, 0, 0, 0)),
                  vec(RW_PROJ), vec(RWKV_WIDTH), mat(D_DECAY_LORA, RWKV_WIDTH), vec(RWKV_WIDTH),
                  mat(D_AAA_LORA, RWKV_WIDTH), mat(D_GATE_LORA, RWKV_WIDTH),
                  vec(RWKV_WIDTH), vec(RWKV_WIDTH), vec(RWKV_WIDTH), vec(RWKV_WIDTH), vec(RWKV_WIDTH)],
        out_specs=[pl.BlockSpec((chunk, RWKV_WIDTH), lambda b, c: (b * n_chunks + c, 0)),
                   pl.BlockSpec((None, RWKV_HEADS, HEAD_DIM, HEAD_DIM), lambda b, c: (b, 0, 0, 0)),
                   pl.BlockSpec((None, 1, RW_PROJ), lambda b, c: (b, 0, 0))],
        out_shape=[jax.ShapeDtypeStruct((n_batch * seq, RWKV_WIDTH), F32),
                   jax.ShapeDtypeStruct((n_batch, RWKV_HEADS, HEAD_DIM, HEAD_DIM), F32),
                   jax.ShapeDtypeStruct((n_batch, 1, RW_PROJ), F32)],
        scratch_shapes=[pltpu.VMEM((RWKV_HEADS, HEAD_DIM, HEAD_DIM), F32),
                        pltpu.VMEM((1, RW_PROJ), F32)],
        compiler_params=_params("parallel", "arbitrary"),
        name="rwkv_mix",
    )(proj, shift_prev, wkv0, lw['rw_mu'], lw['rw_w0'], lw['rw_w2'], lw['rw_a0'], lw['rw_a2'], lw['rw_g2'],
      lw['rw_k_k'], lw['rw_k_a'], lw['rw_r_k'], lw['rw_ln_w'], lw['rw_ln_b'])


def _nsa_prep_kernel(q_ref, kc_ref, ks_ref, kw_ref, cos_ref, sin_ref, qg_ref, kg_ref, bd_ref,
                     qo_ref, kco_ref, kso_ref, kwo_ref):
    cos = cos_ref[...]
    sin = sin_ref[...]
    half = HEAD_DIM // 2

    def norm_rope(x, gain):
        width = x.shape[1]
        ms = _mm2(x * x, bd_ref[:width, :width])
        y = x * lax.rsqrt(ms + RMS_EPS) * gain
        reps = width // cos.shape[1]
        c = jnp.concatenate([cos] * reps, axis=1) if reps > 1 else cos
        s = jnp.concatenate([sin] * reps, axis=1) if reps > 1 else sin
        lane = lax.broadcasted_iota(jnp.int32, (1, width), 1) % HEAD_DIM
        swapped = jnp.where(lane < half, pltpu.roll(y, width - half, 1), pltpu.roll(y, half, 1))
        return y * c + swapped * s

    qo_ref[...] = norm_rope(q_ref[...], qg_ref[...])
    kw_half = NSA_KV_HEADS * HEAD_DIM
    for br, (src, dst) in enumerate(((kc_ref, kco_ref), (ks_ref, kso_ref), (kw_ref, kwo_ref))):
        kv = src[...]
        dst[:, :kw_half] = norm_rope(kv[:, :kw_half], kg_ref[br:br + 1, :])
        dst[:, kw_half:] = kv[:, kw_half:]


def nsa_prep(proj, cos_tab, sin_tab, q_gain, k_gain, bd, tm, tiles_per_seq, n_prompt_tiles):
    n = proj.shape[0]
    tab_idx = lambda i: (jnp.where(i < n_prompt_tiles, i % tiles_per_seq, tiles_per_seq), 0)
    kv_col0 = RW_PROJ // KV_BRANCH
    q_col = (RW_PROJ + 3 * KV_BRANCH) // NSA_WIDTH
    const = lambda shape: pl.BlockSpec(shape, lambda i: (0, 0))
    return pl.pallas_call(
        _nsa_prep_kernel,
        grid=(n // tm,),
        in_specs=[pl.BlockSpec((tm, NSA_WIDTH), lambda i: (i, q_col)),
                  pl.BlockSpec((tm, KV_BRANCH), lambda i: (i, kv_col0)),
                  pl.BlockSpec((tm, KV_BRANCH), lambda i: (i, kv_col0 + 1)),
                  pl.BlockSpec((tm, KV_BRANCH), lambda i: (i, kv_col0 + 2)),
                  pl.BlockSpec((tm, 2 * HEAD_DIM), tab_idx),
                  pl.BlockSpec((tm, 2 * HEAD_DIM), tab_idx),
                  const((1, NSA_WIDTH)), const((3, 2 * HEAD_DIM)), const((NSA_WIDTH, NSA_WIDTH))],
        out_specs=[pl.BlockSpec((tm, NSA_WIDTH), lambda i: (i, 0))] +
                  [pl.BlockSpec((tm, KV_BRANCH), lambda i: (i, 0))] * 3,
        out_shape=[jax.ShapeDtypeStruct((n, NSA_WIDTH), F32)] +
                  [jax.ShapeDtypeStruct((n, KV_BRANCH), F32)] * 3,
        compiler_params=_params("parallel"),
        name="nsa_prep",
    )(proj, proj, proj, proj, cos_tab, sin_tab, q_gain, k_gain, bd)


def _gelu_tanh(x):
    return 0.5 * x * (1.0 + jnp.tanh(math.sqrt(2.0 / math.pi) * (x + 0.044715 * (x * x * x))))


def _compress_body(srcs_k, srcs_v, pek_ref, pev_ref, w1k_ref, w1v_ref, w2k_ref, w2v_ref, o_ref):
    half = NSA_KV_HEADS * HEAD_DIM
    n_grp = sum(s.shape[0] for s in srcs_k) // CMP_STRIDE
    acc = [jnp.zeros((n_grp, NSA_KV_HEADS * CMP_HIDDEN), F32) for _ in range(4)]

    def every_16th(srcs, l):
        parts = [s[pl.ds(l, s.shape[0] // CMP_STRIDE, stride=CMP_STRIDE), :] for s in srcs]
        return jnp.concatenate(parts, axis=0) if len(parts) > 1 else parts[0]

    for l in range(CMP_STRIDE):
        xk, xv = every_16th(srcs_k, l), every_16th(srcs_v, l)
        l2 = CMP_STRIDE + l
        acc[0] += _mm(xk + pek_ref[l:l + 1, :], w1k_ref[l])
        acc[1] += _mm(xk + pek_ref[l2:l2 + 1, :], w1k_ref[l2])
        acc[2] += _mm(xv + pev_ref[l:l + 1, :], w1v_ref[l])
        acc[3] += _mm(xv + pev_ref[l2:l2 + 1, :], w1v_ref[l2])
    pre_k = acc[0] + pltpu.roll(acc[1], n_grp - 1, 0)
    pre_v = acc[2] + pltpu.roll(acc[3], n_grp - 1, 0)
    o_ref[:, :half] = _mm(_gelu_tanh(pre_k), w2k_ref[...])
    o_ref[:, half:] = _mm(_gelu_tanh(pre_v), w2v_ref[...])


def _compress_prompt_kernel(k_ref, v_ref, *rest):
    _compress_body([k_ref], [v_ref], *rest)


def _compress_paged_kernel(pt_ref, *refs, n_pages):
    _compress_body(list(refs[:n_pages]), list(refs[n_pages:2 * n_pages]), *refs[2 * n_pages:])


def _compress_weight_specs(index_map):
    hid2 = NSA_KV_HEADS * CMP_HIDDEN
    half = NSA_KV_HEADS * HEAD_DIM
    return [pl.BlockSpec((CMP_BLOCK, half), index_map(2)), pl.BlockSpec((CMP_BLOCK, half), index_map(2)),
            pl.BlockSpec((CMP_BLOCK, half, hid2), index_map(3)), pl.BlockSpec((CMP_BLOCK, half, hid2), index_map(3)),
            pl.BlockSpec((hid2, half), index_map(2)), pl.BlockSpec((hid2, half), index_map(2))]


def compress_prompt(kvc, n_batch, seq, cw):
    zeros = lambda nd: (lambda b: (0,) * nd)
    n_grp = seq // CMP_STRIDE
    return pl.pallas_call(
        _compress_prompt_kernel,
        grid=(n_batch,),
        in_specs=[pl.BlockSpec((seq, KV_BRANCH // 2), lambda b: (b, 0)),
                  pl.BlockSpec((seq, KV_BRANCH // 2), lambda b: (b, 1))] + _compress_weight_specs(zeros),
        out_specs=pl.BlockSpec((None, n_grp, KV_BRANCH), lambda b: (b, 0, 0)),
        out_shape=jax.ShapeDtypeStruct((n_batch, n_grp, KV_BRANCH), F32),
        compiler_params=_params("parallel"),
        name="compress_prompt",
    )(kvc, kvc, *cw)


def compress_paged(cache, layer, page_table, cw):
    n_batch, n_pages = page_table.shape
    page = cache.shape[2]
    n_grp = n_pages * page // CMP_STRIDE
    zeros = lambda nd: (lambda b, pt: (0,) * nd)
    page_spec = lambda p, kv: pl.BlockSpec((None, None, page, KV_BRANCH // 2),
                                           lambda b, pt: (layer, pt[b, p], 0, kv))
    grid_spec = pltpu.PrefetchScalarGridSpec(
        num_scalar_prefetch=1,
        grid=(n_batch,),
        in_specs=[page_spec(p, 0) for p in range(n_pages)] + [page_spec(p, 1) for p in range(n_pages)] +
                 _compress_weight_specs(zeros),
        out_specs=pl.BlockSpec((None, n_grp, KV_BRANCH), lambda b, pt: (b, 0, 0)),
    )
    return pl.pallas_call(
        functools.partial(_compress_paged_kernel, n_pages=n_pages),
        grid_spec=grid_spec,
        out_shape=jax.ShapeDtypeStruct((n_batch, n_grp, KV_BRANCH), F32),
        compiler_params=_params("parallel"),
        name="compress_paged",
    )(page_table, *([cache] * (2 * n_pages)), *cw)


def _compressed_branch(q, qpos, kc, vc, n_cmp):
    n_grp = kc.shape[0]
    s = _mm3(q, kc, _NT)
    n_idx = lax.broadcasted_iota(jnp.int32, (1, n_grp), 1)
    mask = (n_idx * CMP_STRIDE + (CMP_BLOCK - 1) <= qpos) & (n_idx < n_cmp)
    s = jnp.where(mask, s, NEG)
    e = jnp.where(mask, jnp.exp(s - jnp.max(s, axis=-1, keepdims=True)), 0.0)
    den = jnp.sum(e, axis=-1, keepdims=True)
    p = e / jnp.where(den > 0.0, den, 1.0)
    return _mm(p, vc), p


def _select_blocks(p_sum, qpos, n_cmp, n_slc, width):
    n_grp = p_sum.shape[1]
    n_i = lax.broadcasted_iota(jnp.int32, (n_grp, width), 0) * CMP_STRIDE
    j_i = lax.broadcasted_iota(jnp.int32, (n_grp, width), 1) * SEL_BLOCK
    overlap = (n_i < j_i + SEL_BLOCK) & (n_i + CMP_BLOCK > j_i) & (n_i < n_cmp * CMP_STRIDE)
    imp = _mm2(p_sum, overlap.astype(F32))
    j = lax.broadcasted_iota(jnp.int32, (1, width), 1)
    cur = qpos // SEL_BLOCK
    valid = (j * SEL_BLOCK <= qpos) & (j < n_slc)
    forced = (j == 0) | (j == cur) | (j == cur - 1)
    imp = jnp.where(valid, jnp.where(forced, FORCE_SCORE, imp), NEG)
    rank = jnp.zeros(imp.shape, F32)
    for jp in range(n_slc):
        col = imp[:, jp:jp + 1]
        ahead = (col > imp) | ((col == imp) & (j > jp))
        rank = rank + jnp.where(ahead, 1.0, 0.0)
    return jnp.where((rank < float(min(N_SEL, n_slc))) & valid, 1.0, 0.0)


def _stack_heads(x, g):
    return jnp.concatenate([x[:, (g * NSA_GROUP + r) * HEAD_DIM:(g * NSA_GROUP + r + 1) * HEAD_DIM]
                            for r in range(NSA_GROUP)], axis=0)


def _stack_gate(gsig, g, branch):
    return jnp.concatenate([gsig[:, 3 * (g * NSA_GROUP + r) + branch:3 * (g * NSA_GROUP + r) + branch + 1]
                            for r in range(NSA_GROUP)], axis=0)


def _tile_lanes(x, reps):
    return jnp.concatenate([x] * reps, axis=1)


def _softmax_step_t(carry, s, v):
    m, l, acc = carry
    m_new = jnp.maximum(m, jnp.max(s, axis=0, keepdims=True))
    alpha = jnp.exp(m - m_new)
    p = jnp.exp(s - m_new)
    l = alpha * l + jnp.sum(p, axis=0, keepdims=True)
    acc = alpha * acc + _mm(v, p, _TN)
    return m_new, l, acc


def _softmax_init_t(cols):
    return (jnp.full((1, cols), NEG, F32), jnp.zeros((1, cols), F32), jnp.zeros((HEAD_DIM, cols), F32))


def _compressed_branch_t(q_t, qpos, kc, vc, n_cmp):
    n_grp = kc.shape[0]
    s = _mm3(kc, q_t)
    n_idx = lax.broadcasted_iota(jnp.int32, (n_grp, 1), 0)
    mask = (n_idx * CMP_STRIDE + (CMP_BLOCK - 1) <= qpos) & (n_idx < n_cmp)
    s = jnp.where(mask, s, NEG)
    e = jnp.where(mask, jnp.exp(s - jnp.max(s, axis=0, keepdims=True)), 0.0)
    den = jnp.sum(e, axis=0, keepdims=True)
    p = e / jnp.where(den > 0.0, den, 1.0)
    return _mm(vc, p, _TN), p


def _select_blocks_t(p_sum, qpos, n_cmp, n_slc, height):
    n_grp = p_sum.shape[0]
    j_i = lax.broadcasted_iota(jnp.int32, (height, n_grp), 0) * SEL_BLOCK
    n_i = lax.broadcasted_iota(jnp.int32, (height, n_grp), 1) * CMP_STRIDE
    overlap = (n_i < j_i + SEL_BLOCK) & (n_i + CMP_BLOCK > j_i) & (n_i < n_cmp * CMP_STRIDE)
    imp = _mm2r(overlap.astype(F32), p_sum)
    j = lax.broadcasted_iota(jnp.int32, (height, 1), 0)
    cur = qpos // SEL_BLOCK
    valid = (j * SEL_BLOCK <= qpos) & (j < n_slc)
    forced = (j == 0) | (j == cur) | (j == cur - 1)
    imp = jnp.where(valid, jnp.where(forced, FORCE_SCORE, imp), NEG)
    rank = jnp.zeros(imp.shape, F32)
    for jp in range(n_slc):
        row = imp[jp:jp + 1, :]
        ahead = (row > imp) | ((row == imp) & (j > jp))
        rank = rank + jnp.where(ahead, 1.0, 0.0)
    return jnp.where((rank < float(min(N_SEL, n_slc))) & valid, 1.0, 0.0)


def _attn_prompt_kernel(q_ref, gate_ref, cmp_ref, ks_ref, kw_ref, o_ref, *, seq, tq):
    i = pl.program_id(1)
    scale = HEAD_DIM ** -0.5
    kblock = 2 * tq
    q_t_all = (q_ref[...] * scale).T
    gate_t = _sigmoid(gate_ref[...]).T
    n_cmp = (seq - CMP_BLOCK) // CMP_STRIDE + 1
    n_slc = -(-seq // SEL_BLOCK)
    qpos1 = i * tq + lax.broadcasted_iota(jnp.int32, (1, tq), 1)
    qpos = _tile_lanes(qpos1, NSA_GROUP)
    cols = NSA_GROUP * tq
    half = NSA_KV_HEADS * HEAD_DIM
    key_i = lax.broadcasted_iota(jnp.int32, (kblock, 1), 0)
    kv_slices = [(slice(g * HEAD_DIM, (g + 1) * HEAD_DIM), slice(half + g * HEAD_DIM, half + (g + 1) * HEAD_DIM))
                 for g in range(NSA_KV_HEADS)]
    q_t, o_c, sel_t = [], [], []
    for g, (gs, vs) in enumerate(kv_slices):
        qg = jnp.concatenate([q_t_all[(g * NSA_GROUP + r) * HEAD_DIM:(g * NSA_GROUP + r + 1) * HEAD_DIM, :]
                              for r in range(NSA_GROUP)], axis=1)
        oc, p_c = _compressed_branch_t(qg, qpos, cmp_ref[:, gs], cmp_ref[:, vs], n_cmp)
        p_sum = p_c[:, 0:tq]
        for r in range(1, NSA_GROUP):
            p_sum = p_sum + p_c[:, r * tq:(r + 1) * tq]
        q_t.append(qg.astype(BF16))
        o_c.append(oc)
        sel_t.append(_select_blocks_t(p_sum, qpos1, n_cmp, n_slc, n_slc).astype(BF16))

    def branch_step(ref, kb, bias, carries):
        start = pl.multiple_of(kb * kblock, kblock)
        out = []
        for g, (gs, vs) in enumerate(kv_slices):
            s = _dg(ref[pl.ds(start, kblock), gs].astype(BF16), q_t[g], _NN)
            out.append(_softmax_step_t(carries[g], s + _tile_lanes(bias[g], NSA_GROUP), ref[pl.ds(start, kblock), vs]))
        return tuple(out)

    def sel_body(kb, carries):
        kpos = kb * kblock + key_i
        blk = lax.broadcasted_iota(jnp.int32, (kblock, n_slc), 0) // SEL_BLOCK + kb * (kblock // SEL_BLOCK)
        expand = jnp.where(lax.broadcasted_iota(jnp.int32, (kblock, n_slc), 1) == blk, 1.0, 0.0).astype(BF16)
        bias = [jnp.where((kpos <= qpos1) & (_dg(expand, sel_t[g], _NN) > 0.5), 0.0, NEG)
                for g in range(NSA_KV_HEADS)]
        return branch_step(ks_ref, kb, bias, carries)

    init = tuple(_softmax_init_t(cols) for _ in range(NSA_KV_HEADS))
    last = (i * tq) // kblock
    sel_out = lax.fori_loop(0, last + 1, sel_body, init)

    win_out = init
    n_win = WINDOW // kblock + 1
    for w in range(n_win):
        kb = last - (n_win - 1) + w
        dpos = qpos1 - (kb * kblock + key_i)
        bias1 = jnp.where((dpos >= 0) & (dpos < WINDOW) & (kb >= 0), 0.0, NEG)
        win_out = branch_step(kw_ref, jnp.maximum(kb, 0), [bias1] * NSA_KV_HEADS, win_out)

    rows = []
    for g in range(NSA_KV_HEADS):
        gate = lambda br: jnp.concatenate([gate_t[3 * (g * NSA_GROUP + r) + br:3 * (g * NSA_GROUP + r) + br + 1, :]
                                           for r in range(NSA_GROUP)], axis=1)
        o_t = (o_c[g] * gate(0) + sel_out[g][2] / sel_out[g][1] * gate(1) + win_out[g][2] / win_out[g][1] * gate(2))
        rows += [o_t[:, r * tq:(r + 1) * tq] for r in range(NSA_GROUP)]
    o_ref[...] = jnp.concatenate(rows, axis=0).T


def attn_prompt(q_rot, proj, cmp_kv, kvs, kvw, n_batch, seq, tq):
    n_qb = seq // tq
    gate_col = (RW_PROJ + 3 * KV_BRANCH + NSA_WIDTH) // GATE_PAD
    n_grp = seq // CMP_STRIDE
    return pl.pallas_call(
        functools.partial(_attn_prompt_kernel, seq=seq, tq=tq),
        grid=(n_batch, n_qb),
        in_specs=[pl.BlockSpec((tq, NSA_WIDTH), lambda b, i: (b * n_qb + i, 0)),
                  pl.BlockSpec((tq, GATE_PAD), lambda b, i: (b * n_qb + i, gate_col)),
                  pl.BlockSpec((None, n_grp, KV_BRANCH), lambda b, i: (b, 0, 0)),
                  pl.BlockSpec((seq, KV_BRANCH), lambda b, i: (b, 0)),
                  pl.BlockSpec((seq, KV_BRANCH), lambda b, i: (b, 0))],
        out_specs=pl.BlockSpec((tq, NSA_WIDTH), lambda b, i: (b * n_qb + i, 0)),
        out_shape=jax.ShapeDtypeStruct((n_batch * seq, NSA_WIDTH), F32),
        compiler_params=_params("parallel", "arbitrary"),
        name="attn_prompt",
    )(q_rot, proj, cmp_kv, kvs, kvw)


def _pad_rows(x, rows):
    return jnp.concatenate([x, jnp.zeros((rows - x.shape[0], x.shape[1]), x.dtype)], axis=0)


def _row_softmax_attend(q_bf, k_tiles, v_tiles, masks):
    s = [jnp.where(mk, _dg(q_bf, kt.astype(BF16), _NT), NEG) for kt, mk in zip(k_tiles, masks)]
    m = functools.reduce(jnp.maximum, s)
    m = jnp.max(m, axis=-1, keepdims=True)
    e = [jnp.exp(t - m) for t in s]
    l = jnp.sum(functools.reduce(jnp.add, e), axis=-1, keepdims=True)
    acc = functools.reduce(jnp.add, [_mm(et, vt) for et, vt in zip(e, v_tiles)])
    return acc / l


def _attn_sample_kernel(pt_ref, q_ref, gate_ref, cmp_ref, expand_ref, *refs, n_pages, past, win_buf):
    pages = refs[:n_pages]
    snew_ref, win_ref, wnew_ref, o_ref, winout_ref = refs[n_pages:]
    tq = q_ref.shape[0]
    page = pages[0].shape[0]
    scale = HEAD_DIM ** -0.5
    q_all = q_ref[...] * scale
    gsig = _sigmoid(gate_ref[...])
    seq = past + tq
    n_cmp = (seq - CMP_BLOCK) // CMP_STRIDE + 1
    n_slc = -(-seq // SEL_BLOCK)
    width = 128 * (-(-n_slc // 128))
    qpos1 = past + lax.broadcasted_iota(jnp.int32, (tq, 1), 0)
    qpos = jnp.concatenate([qpos1] * NSA_GROUP, axis=0)
    rows = NSA_GROUP * tq
    half = NSA_KV_HEADS * HEAD_DIM
    key_i = lax.broadcasted_iota(jnp.int32, (1, KEY_BLOCK), 1)
    snew = _pad_rows(snew_ref[...], KEY_BLOCK)
    wnew = _pad_rows(wnew_ref[...], KEY_BLOCK)
    for g in range(NSA_KV_HEADS):
        gs = slice(g * HEAD_DIM, (g + 1) * HEAD_DIM)
        vs = slice(half + g * HEAD_DIM, half + (g + 1) * HEAD_DIM)
        q = _stack_heads(q_all, g)
        o_c, p_c = _compressed_branch(q, qpos, cmp_ref[:, gs], cmp_ref[:, vs], n_cmp)
        p_sum = p_c[0:tq]
        for r in range(1, NSA_GROUP):
            p_sum = p_sum + p_c[r * tq:(r + 1) * tq]
        sel = _select_blocks(p_sum, qpos1, n_cmp, n_slc, width)
        sel4 = jnp.concatenate([sel] * NSA_GROUP, axis=0).astype(BF16)
        picked = _dg(sel4, expand_ref[...], _NN)
        q_bf = q.astype(BF16)
        k_tiles, v_tiles, masks = [], [], []
        for p in range(n_pages):
            for kb in range(page // KEY_BLOCK):
                start = p * page + kb * KEY_BLOCK
                k_tiles.append(pages[p][kb * KEY_BLOCK:(kb + 1) * KEY_BLOCK, gs])
                v_tiles.append(pages[p][kb * KEY_BLOCK:(kb + 1) * KEY_BLOCK, vs])
                masks.append((start + key_i <= qpos) & (picked[:, start:start + KEY_BLOCK] > 0.5))
        k_tiles.append(snew[:, gs])
        v_tiles.append(snew[:, vs])
        masks.append((past + key_i <= qpos) & (picked[:, past:past + KEY_BLOCK] > 0.5) & (key_i < tq))
        o_s = _row_softmax_attend(q_bf, k_tiles, v_tiles, masks)
        k_tiles, v_tiles, masks = [], [], []
        for kb in range(win_buf // KEY_BLOCK):
            kpos = past - win_buf + kb * KEY_BLOCK + key_i
            dpos = qpos - kpos
            k_tiles.append(win_ref[kb * KEY_BLOCK:(kb + 1) * KEY_BLOCK, gs])
            v_tiles.append(win_ref[kb * KEY_BLOCK:(kb + 1) * KEY_BLOCK, vs])
            masks.append((kpos >= 0) & (dpos >= 0) & (dpos < WINDOW))
        dpos = qpos - (past + key_i)
        k_tiles.append(wnew[:, gs])
        v_tiles.append(wnew[:, vs])
        masks.append((dpos >= 0) & (dpos < WINDOW) & (key_i < tq))
        o_w = _row_softmax_attend(q_bf, k_tiles, v_tiles, masks)
        o = (o_c * _stack_gate(gsig, g, 0) + o_s * _stack_gate(gsig, g, 1) + o_w * _stack_gate(gsig, g, 2))
        for r in range(NSA_GROUP):
            h = g * NSA_GROUP + r
            o_ref[:, h * HEAD_DIM:(h + 1) * HEAD_DIM] = o[r * tq:(r + 1) * tq]
    winout_ref[0:win_buf - tq, :] = win_ref[tq:win_buf, :]
    winout_ref[win_buf - tq:win_buf, :] = wnew_ref[...]


def attn_sample(q_rot, proj, cmp_kv, cache_sel, layer, page_table, kvs, win_state, kvw, row0, tq):
    n_batch, n_pages = page_table.shape
    page = cache_sel.shape[2]
    past = n_pages * page
    win_buf = win_state.shape[2]
    blk0 = row0 // tq
    gate_col = (RW_PROJ + 3 * KV_BRANCH + NSA_WIDTH) // GATE_PAD
    n_grp = cmp_kv.shape[1]
    n_slc = -(-(past + tq) // SEL_BLOCK)
    width = 128 * (-(-n_slc // 128))
    expand = jnp.asarray(np.arange(width)[:, None] == np.arange(past + KEY_BLOCK)[None, :] // SEL_BLOCK, BF16)
    page_spec = lambda p: pl.BlockSpec((None, None, page, KV_BRANCH), lambda b, pt: (layer, pt[b, p], 0, 0))
    grid_spec = pltpu.PrefetchScalarGridSpec(
        num_scalar_prefetch=1,
        grid=(n_batch,),
        in_specs=[pl.BlockSpec((tq, NSA_WIDTH), lambda b, pt: (blk0 + b, 0)),
                  pl.BlockSpec((tq, GATE_PAD), lambda b, pt: (blk0 + b, gate_col)),
                  pl.BlockSpec((None, n_grp, KV_BRANCH), lambda b, pt: (b, 0, 0)),
                  pl.BlockSpec(expand.shape, lambda b, pt: (0, 0))] +
                 [page_spec(p) for p in range(n_pages)] +
                 [pl.BlockSpec((tq, KV_BRANCH), lambda b, pt: (blk0 + b, 0)),
                  pl.BlockSpec((None, None, win_buf, KV_BRANCH), lambda b, pt: (layer, b, 0, 0)),
                  pl.BlockSpec((tq, KV_BRANCH), lambda b, pt: (blk0 + b, 0))],
        out_specs=[pl.BlockSpec((tq, NSA_WIDTH), lambda b, pt: (b, 0)),
                   pl.BlockSpec((None, win_buf, KV_BRANCH), lambda b, pt: (b, 0, 0))],
    )
    return pl.pallas_call(
        functools.partial(_attn_sample_kernel, n_pages=n_pages, past=past, win_buf=win_buf),
        grid_spec=grid_spec,
        out_shape=[jax.ShapeDtypeStruct((n_batch * tq, NSA_WIDTH), F32),
                   jax.ShapeDtypeStruct((n_batch, win_buf, KV_BRANCH), F32)],
        compiler_params=_params("parallel"),
        name="attn_sample",
    )(page_table, q_rot, proj, cmp_kv, expand, *([cache_sel] * n_pages), kvs, win_state, kvw)


def _out_proj_kernel(x_ref, yp_ref, ys_ref, op_ref, os_ref, g_ref, wa_ref, wb_ref, out_ref, *, n_prompt_tiles):
    is_prompt = pl.program_id(0) < n_prompt_tiles
    y = jnp.where(is_prompt, yp_ref[...], ys_ref[...])
    o = _rms(jnp.where(is_prompt, op_ref[...], os_ref[...]), g_ref[...])
    out_ref[...] = x_ref[...] + _dg(y.astype(BF16), wa_ref[...], _NN) + _dg(o.astype(BF16), wb_ref[...], _NN)


def out_proj(x, y_p, y_s, o_p, o_s, gain, w_a, w_b, tm):
    n, d = x.shape
    npt = y_p.shape[0] // tm
    p_idx = lambda i: (jnp.minimum(i, npt - 1), 0)
    s_idx = lambda i: (jnp.maximum(i - npt, 0), 0)
    const = lambda shape: pl.BlockSpec(shape, lambda i: (0, 0))
    return pl.pallas_call(
        functools.partial(_out_proj_kernel, n_prompt_tiles=npt),
        grid=(n // tm,),
        in_specs=[pl.BlockSpec((tm, d), lambda i: (i, 0)),
                  pl.BlockSpec((tm, RWKV_WIDTH), p_idx), pl.BlockSpec((tm, RWKV_WIDTH), s_idx),
                  pl.BlockSpec((tm, NSA_WIDTH), p_idx), pl.BlockSpec((tm, NSA_WIDTH), s_idx),
                  const((1, NSA_WIDTH)), const((RWKV_WIDTH, d)), const((NSA_WIDTH, d))],
        out_specs=pl.BlockSpec((tm, d), lambda i: (i, 0)),
        out_shape=jax.ShapeDtypeStruct((n, d), F32),
        compiler_params=_params("parallel"),
        name="out_proj",
    )(x, y_p, y_s, o_p, o_s, gain, w_a, w_b)


def _swiglu_partial(h, wg, wu, wd):
    a = _dg(h, wg, _NN)
    u = _dg(h, wu, _NN)
    return _dg((a * _sigmoid(a) * u).astype(BF16), wd, _NN)


def _ffn_kernel(x_ref, g_ref, wg_ref, wu_ref, wd_ref, o_ref, h_scr, acc_scr, *, n_f):
    f = pl.program_id(1)

    @pl.when(f == 0)
    def _():
        x = x_ref[...]
        h_scr[...] = _rms(x, g_ref[...]).astype(BF16)
        acc_scr[...] = x

    acc_scr[...] += _swiglu_partial(h_scr[...], wg_ref[...], wu_ref[...], wd_ref[...])

    @pl.when(f == n_f - 1)
    def _():
        o_ref[...] = acc_scr[...]


def ffn(x, gain, wg, wu, wd, tm, tf):
    n, d = x.shape
    n_f = wg.shape[1] // tf
    return pl.pallas_call(
        functools.partial(_ffn_kernel, n_f=n_f),
        grid=(n // tm, n_f),
        in_specs=[pl.BlockSpec((tm, d), lambda i, f: (i, 0)),
                  pl.BlockSpec((1, d), lambda i, f: (0, 0)),
                  pl.BlockSpec((d, tf), lambda i, f: (0, f)),
                  pl.BlockSpec((d, tf), lambda i, f: (0, f)),
                  pl.BlockSpec((tf, d), lambda i, f: (f, 0))],
        out_specs=pl.BlockSpec((tm, d), lambda i, f: (i, 0)),
        out_shape=jax.ShapeDtypeStruct((n, d), F32),
        scratch_shapes=[pltpu.VMEM((tm, d), BF16), pltpu.VMEM((tm, d), F32)],
        compiler_params=_params("parallel", "arbitrary"),
        name="ffn",
    )(x, gain, wg, wu, wd)


def _route_kernel(x_ref, g_ref, wr_ref, br_ref, o_ref, mask_ref):
    h = _rms(x_ref[...], g_ref[...])
    logits = _mm3(h, wr_ref[...]) + br_ref[...]
    lane = lax.broadcasted_iota(jnp.int32, logits.shape, 1)
    low = -3.0e38
    lg = jnp.where(lane < N_EXPERTS, logits, low)
    m1 = jnp.max(lg, axis=-1, keepdims=True)
    i1 = jnp.min(jnp.where(lg == m1, lane, lg.shape[1]), axis=-1, keepdims=True)
    lg2 = jnp.where(lane == i1, low, lg)
    m2 = jnp.max(lg2, axis=-1, keepdims=True)
    i2 = jnp.min(jnp.where(lg2 == m2, lane, lg.shape[1]), axis=-1, keepdims=True)
    e2 = jnp.exp(m2 - m1)
    den = 1.0 + e2
    o_ref[...] = jnp.where(lane == i1, 1.0 / den, 0.0) + jnp.where(lane == i2, e2 / den, 0.0)
    mask_ref[...] = jnp.where((lane == i1) | (lane == i2), 1.0, 0.0)


def moe_route(x, gain, w_router, b_router, tm):
    n, d = x.shape
    pad = w_router.shape[1]
    return pl.pallas_call(
        _route_kernel,
        grid=(n // tm,),
        in_specs=[pl.BlockSpec((tm, d), lambda i: (i, 0)),
                  pl.BlockSpec((1, d), lambda i: (0, 0)),
                  pl.BlockSpec((d, pad), lambda i: (0, 0)),
                  pl.BlockSpec((1, pad), lambda i: (0, 0))],
        out_specs=[pl.BlockSpec((tm, pad), lambda i: (i, 0))] * 2,
        out_shape=[jax.ShapeDtypeStruct((n, pad), F32)] * 2,
        compiler_params=_params("parallel"),
        name="moe_route",
    )(x, gain, w_router, b_router)


MOE_TILE = 1024
MOE_CAP = 384


def _place(slot_col, cap):
    r = lax.broadcasted_iota(jnp.int32, (1, cap), 1).astype(F32)
    return jnp.where(slot_col == r, 1.0, 0.0).astype(BF16)


def _moe_gather_kernel(x_ref, g_ref, mask_ref, gates_ref, hg_ref, gr_ref, slot_ref, *, cap):
    h = _rms(x_ref[...], g_ref[...]).astype(BF16)
    mask = mask_ref[...]
    t = mask.shape[0]
    tri = jnp.where(lax.broadcasted_iota(jnp.int32, (t, t), 0) >= lax.broadcasted_iota(jnp.int32, (t, t), 1),
                    1.0, 0.0).astype(BF16)
    incl = _dg(tri, mask.astype(BF16), _NN)
    slot = jnp.where(mask > 0.5, incl - 1.0, -1.0)
    slot_ref[...] = slot
    gates = gates_ref[...]
    for e in range(N_EXPERTS):
        place = _place(slot[:, e:e + 1], cap)
        hg_ref[e] = _dg(place, h, _TN).astype(BF16)
        gr_ref[e] = _mm2r(place, gates, _TN)


def moe_gather(x, gain, mask, gates, tile, cap):
    n, d = x.shape
    n_tiles = n // tile
    pad = mask.shape[1]
    return pl.pallas_call(
        functools.partial(_moe_gather_kernel, cap=cap),
        grid=(n_tiles,),
        in_specs=[pl.BlockSpec((tile, d), lambda i: (i, 0)),
                  pl.BlockSpec((1, d), lambda i: (0, 0)),
                  pl.BlockSpec((tile, pad), lambda i: (i, 0)),
                  pl.BlockSpec((tile, pad), lambda i: (i, 0))],
        out_specs=[pl.BlockSpec((None, N_EXPERTS, cap, d), lambda i: (i, 0, 0, 0)),
                   pl.BlockSpec((None, N_EXPERTS, cap, pad), lambda i: (i, 0, 0, 0)),
                   pl.BlockSpec((tile, pad), lambda i: (i, 0))],
        out_shape=[jax.ShapeDtypeStruct((n_tiles, N_EXPERTS, cap, d), BF16),
                   jax.ShapeDtypeStruct((n_tiles, N_EXPERTS, cap, pad), F32),
                   jax.ShapeDtypeStruct((n, pad), F32)],
        compiler_params=_params("parallel"),
        name="moe_gather",
    )(x, gain, mask, gates)


def _moe_routed_kernel(x_ref, slot_ref, hg_ref, gr_ref, wg_ref, wu_ref, wd_ref, o_ref, y_scr, *, n_f):
    e = pl.program_id(1)
    f = pl.program_id(2)

    @pl.when((e == 0) & (f == 0))
    def _():
        o_ref[...] = x_ref[...]

    part = _swiglu_partial(hg_ref[...], wg_ref[...], wu_ref[...], wd_ref[...])

    @pl.when(f == 0)
    def _():
        y_scr[...] = part

    @pl.when(f != 0)
    def _():
        y_scr[...] += part

    @pl.when(f == n_f - 1)
    def _():
        gr = gr_ref[...]
        gate_rows = jnp.sum(jnp.where(lax.broadcasted_iota(jnp.int32, gr.shape, 1) == e, gr, 0.0),
                            axis=-1, keepdims=True)
        slot = slot_ref[...]
        slot_col = jnp.sum(jnp.where(lax.broadcasted_iota(jnp.int32, slot.shape, 1) == e, slot, 0.0),
                           axis=-1, keepdims=True)
        o_ref[...] += _dg(_place(slot_col, y_scr.shape[0]), (y_scr[...] * gate_rows).astype(BF16), _NN)


def moe_routed(x, slot, hg, gr, wg, wu, wd, tile, tf):
    n, d = x.shape
    n_tiles, n_e, cap, _ = hg.shape
    n_f = wg.shape[2] // tf
    pad = slot.shape[1]
    return pl.pallas_call(
        functools.partial(_moe_routed_kernel, n_f=n_f),
        grid=(n_tiles, n_e, n_f),
        in_specs=[pl.BlockSpec((tile, d), lambda i, e, f: (i, 0)),
                  pl.BlockSpec((tile, pad), lambda i, e, f: (i, 0)),
                  pl.BlockSpec((None, None, cap, d), lambda i, e, f: (i, e, 0, 0)),
                  pl.BlockSpec((None, None, cap, pad), lambda i, e, f: (i, e, 0, 0)),
                  pl.BlockSpec((None, d, tf), lambda i, e, f: (e, 0, f)),
                  pl.BlockSpec((None, d, tf), lambda i, e, f: (e, 0, f)),
                  pl.BlockSpec((None, tf, d), lambda i, e, f: (e, f, 0))],
        out_specs=pl.BlockSpec((tile, d), lambda i, e, f: (i, 0)),
        out_shape=jax.ShapeDtypeStruct((n, d), F32),
        scratch_shapes=[pltpu.VMEM((cap, d), F32)],
        compiler_params=_params("parallel", "arbitrary", "arbitrary"),
        name="moe_routed",
    )(x, slot, hg, gr, wg, wu, wd)


def _moe_kernel(x_ref, g_ref, gate_ref, wg_ref, wu_ref, wd_ref, o_ref, h_scr, acc_scr, *, n_e, n_f):
    e = pl.program_id(1)
    f = pl.program_id(2)

    @pl.when((e == 0) & (f == 0))
    def _():
        x = x_ref[...]
        h_scr[...] = _rms(x, g_ref[...]).astype(BF16)
        acc_scr[...] = x

    gates = gate_ref[...]
    lane = lax.broadcasted_iota(jnp.int32, gates.shape, 1)
    gate_e = jnp.sum(jnp.where(lane == e, gates, 0.0), axis=-1, keepdims=True)
    acc_scr[...] += _swiglu_partial(h_scr[...], wg_ref[...], wu_ref[...], wd_ref[...]) * gate_e

    @pl.when((e == n_e - 1) & (f == n_f - 1))
    def _():
        o_ref[...] = acc_scr[...]


def moe(x, gain, gates, wg, wu, wd, tm, tf):
    n, d = x.shape
    n_e = wg.shape[0]
    n_f = wg.shape[2] // tf
    return pl.pallas_call(
        functools.partial(_moe_kernel, n_e=n_e, n_f=n_f),
        grid=(n // tm, n_e, n_f),
        in_specs=[pl.BlockSpec((tm, d), lambda i, e, f: (i, 0)),
                  pl.BlockSpec((1, d), lambda i, e, f: (0, 0)),
                  pl.BlockSpec((tm, gates.shape[1]), lambda i, e, f: (i, 0)),
                  pl.BlockSpec((None, d, tf), lambda i, e, f: (e, 0, f)),
                  pl.BlockSpec((None, d, tf), lambda i, e, f: (e, 0, f)),
                  pl.BlockSpec((None, tf, d), lambda i, e, f: (e, f, 0))],
        out_specs=pl.BlockSpec((tm, d), lambda i, e, f: (i, 0)),
        out_shape=jax.ShapeDtypeStruct((n, d), F32),
        scratch_shapes=[pltpu.VMEM((tm, d), BF16), pltpu.VMEM((tm, d), F32)],
        compiler_params=_params("parallel", "arbitrary", "arbitrary"),
        name="moe",
    )(x, gain, gates, wg, wu, wd)


def _rope_tables(pos):
    half = HEAD_DIM // 2
    inv = ROPE_THETA ** (-jnp.arange(half, dtype=F32) / half)
    ang = pos.astype(F32)[:, None] * inv[None, :]
    cos, sin = jnp.cos(ang), jnp.sin(ang)
    return jnp.tile(jnp.concatenate([cos, cos], axis=1), (1, 2)), jnp.tile(jnp.concatenate([-sin, sin], axis=1), (1, 2))


def _block_diag2(w):
    z = jnp.zeros_like(w)
    return jnp.concatenate([jnp.concatenate([w, z], axis=-1), jnp.concatenate([z, w], axis=-1)], axis=-2)


def kernel(x_prompt, x_sample, cache_cmp_kv, cache_sel_kv, state_win_kv, state_wkv, state_shift, page_table,
           g_attn, w_in, w_out, rw_mu, rw_w0, rw_w2, rw_a0, rw_a2, rw_g2, rw_k_k, rw_k_a, rw_r_k, rw_ln_w, rw_ln_b,
           nsa_q_gain, nsa_k_gain, cmp_pe_k, cmp_w1_k, cmp_w2_k, cmp_pe_v, cmp_w1_v, cmp_w2_v, nsa_out_gain,
           g_ffn, ffn_w_gate, ffn_w_up, ffn_w_down, moe_router, moe_b_router, moe_w_gate, moe_w_up, moe_w_down):
    bp, tp, d = x_prompt.shape
    bs, ts, _ = x_sample.shape
    depth = w_in.shape[0]
    n_p, n_s = bp * tp, bs * ts
    n_phys, page = cache_cmp_kv.shape[1], cache_cmp_kv.shape[2]
    past = page_table.shape[1] * page
    win_buf = state_win_kv.shape[2]
    tm = 512
    x = jnp.concatenate([x_prompt.reshape(n_p, d), x_sample.reshape(n_s, d)], axis=0)
    cache_c = cache_cmp_kv.reshape(depth, n_phys, page, KV_BRANCH)
    cache_s = cache_sel_kv.reshape(depth, n_phys, page, KV_BRANCH)
    win_state = state_win_kv.reshape(depth, bs, win_buf, KV_BRANCH)
    pos = jnp.concatenate([jnp.arange(tp, dtype=jnp.int32), past + jnp.arange(tm, dtype=jnp.int32) % ts])
    cos_tab, sin_tab = _rope_tables(pos)
    head_mean = jnp.kron(jnp.eye(NSA_HEADS, dtype=F32), jnp.full((HEAD_DIM, HEAD_DIM), 1.0 / HEAD_DIM, F32))
    shift0 = jnp.zeros((bp, 1, RW_PROJ), F32)
    wkv0 = jnp.zeros((bp, RWKV_HEADS, HEAD_DIM, HEAD_DIM), F32)
    o_rw, o_kv, o_q = RW_PROJ, RW_PROJ + NSA_WIDTH, RW_PROJ + NSA_WIDTH + 3 * KV_BRANCH
    row = lambda v: v.reshape(1, -1)
    st_p = ([], [], [], [], [])
    st_s = ([], [], [], [], [])
    for l in range(depth):
        wl = w_in[l]
        w_in_l = jnp.concatenate([wl[:, :o_rw], wl[:, o_kv:o_q], wl[:, o_rw:o_kv], wl[:, o_q:],
                                  jnp.zeros((d, GATE_PAD - N_GATE), F32)], axis=1).astype(BF16)
        lw = {'rw_mu': row(rw_mu[l]), 'rw_w0': row(rw_w0[l]), 'rw_w2': rw_w2[l], 'rw_a0': row(rw_a0[l]),
              'rw_a2': rw_a2[l], 'rw_g2': rw_g2[l], 'rw_k_k': row(rw_k_k[l]), 'rw_k_a': row(rw_k_a[l]),
              'rw_r_k': row(rw_r_k[l]), 'rw_ln_w': row(rw_ln_w[l]), 'rw_ln_b': row(rw_ln_b[l])}
        proj = norm_matmul(x, row(g_attn[l]), w_in_l, tm)
        y_p, wkv_p, shift_p = rwkv_mix(proj, 0, bp, tp, RWKV_CHUNK, shift0, wkv0, lw)
        y_s, wkv_s, shift_s = rwkv_mix(proj, n_p, bs, ts, ts, state_shift[l][:, None, :], state_wkv[l], lw)
        q_rot, kvc, kvs, kvw = nsa_prep(proj, cos_tab, sin_tab, jnp.tile(row(nsa_q_gain[l]), (1, NSA_HEADS)),
                                        jnp.tile(nsa_k_gain[l], (1, NSA_KV_HEADS)), head_mean, tm, tp // tm, n_p // tm)
        cw = (jnp.tile(cmp_pe_k[l], (1, NSA_KV_HEADS)), jnp.tile(cmp_pe_v[l], (1, NSA_KV_HEADS)),
              _block_diag2(cmp_w1_k[l].reshape(CMP_BLOCK, HEAD_DIM, CMP_HIDDEN)).astype(BF16),
              _block_diag2(cmp_w1_v[l].reshape(CMP_BLOCK, HEAD_DIM, CMP_HIDDEN)).astype(BF16),
              _block_diag2(cmp_w2_k[l]).astype(BF16), _block_diag2(cmp_w2_v[l]).astype(BF16))
        cmp_p = compress_prompt(kvc, bp, tp, cw)
        cmp_s = compress_paged(cache_c, l, page_table, cw)
        o_p = attn_prompt(q_rot, proj, cmp_p, kvs, kvw, bp, tp, KEY_BLOCK)
        o_s, win_s = attn_sample(q_rot, proj, cmp_s, cache_s, l, page_table, kvs, win_state, kvw, n_p, ts)
        wo = w_out[l].astype(BF16)
        x = out_proj(x, y_p, y_s, o_p, o_s, row(nsa_out_gain[l]), wo[:RWKV_WIDTH], wo[RWKV_WIDTH:], tm)
        i = l // 2
        if l % 2 == 0:
            x = ffn(x, row(g_ffn[l]), ffn_w_gate[i].astype(BF16), ffn_w_up[i].astype(BF16),
                    ffn_w_down[i].astype(BF16), tm, ffn_w_gate.shape[2] // 2)
        else:
            pad = GATE_PAD - N_EXPERTS
            gates, mask = moe_route(x, row(g_ffn[l]), jnp.pad(moe_router[i], ((0, 0), (0, pad))),
                                    jnp.pad(row(moe_b_router[i]), ((0, 0), (0, pad))), tm)
            ew = (moe_w_gate[i].astype(BF16), moe_w_up[i].astype(BF16), moe_w_down[i].astype(BF16))
            gain, tf = row(g_ffn[l]), moe_w_gate.shape[3] // 2

            def routed(x, gates, mask, gain, ew):
                hg, gr, slot = moe_gather(x, gain, mask, gates, MOE_TILE, MOE_CAP)
                return moe_routed(x, slot, hg, gr, *ew, MOE_TILE, tf)

            def dense(x, gates, mask, gain, ew):
                return moe(x, gain, gates, *ew, tm, tf)

            per_tile = jnp.sum(mask.reshape((n_p + n_s) // MOE_TILE, MOE_TILE, GATE_PAD), axis=1)
            x = lax.cond(jnp.max(per_tile) <= MOE_CAP, routed, dense, x, gates, mask, gain, ew)
        kv5 = lambda a, b, t: a.reshape(b, t, 2, NSA_KV_HEADS, HEAD_DIM)
        new_p = (kv5(kvc[:n_p], bp, tp), kv5(kvs[:n_p], bp, tp),
                 kv5(kvw[:n_p], bp, tp)[:, -min(WINDOW, tp):], wkv_p, shift_p[:, 0])
        new_s = (kv5(kvc[n_p:], bs, ts), kv5(kvs[n_p:], bs, ts), kv5(win_s, bs, win_buf), wkv_s, shift_s[:, 0])
        for buf, val in zip(st_p, new_p):
            buf.append(val)
        for buf, val in zip(st_s, new_s):
            buf.append(val)
    outs_p = [jnp.stack(b, axis=0) for b in st_p]
    outs_s = [jnp.stack(b, axis=0) for b in st_s]
    return (x[:n_p].reshape(bp, tp, d), x[n_p:].reshape(bs, ts, d), *outs_p, *outs_s)
```
